```python
import math
import jax, jax.numpy as jnp
from jax import lax
import numpy as np

D_MODEL = 1024
BATCH = 1
SEQ = 16384
DEPTH = 1

CHUNK = 64
ATTN_HEAD_DIM = 64
ATTN_HEADS = (D_MODEL // 2) // ATTN_HEAD_DIM
ATTN_WIDTH = ATTN_HEADS * ATTN_HEAD_DIM
LEFT_CHUNKS = 8
BAND_CHUNKS = LEFT_CHUNKS + 1
REL_CLIP = 128
DN_HEAD_DIM = 128
DN_HEADS = (D_MODEL - ATTN_WIDTH) // DN_HEAD_DIM
DN_WIDTH = DN_HEADS * DN_HEAD_DIM
CONV_K = 4
FFN_HIDDEN = -(-8 * D_MODEL // (3 * 256)) * 256
IN_COLS = 3 * ATTN_WIDTH + 4 * DN_WIDTH + 2 * DN_HEADS
EPS = 1e-6

kernel_name = 'hymba_chunked_attn_gated_deltanet_block'


def rms_norm(x, g):
    xf = x.astype(jnp.float32)
    y = xf * lax.rsqrt(jnp.mean(xf * xf, axis=-1, keepdims=True) + EPS)
    return (y * g.astype(jnp.float32)).astype(x.dtype)


def l2norm(t):
    return t * lax.rsqrt(jnp.sum(t * t, axis=-1, keepdims=True) + EPS)


def chunked_band_attention(q, k, v, q_g, k_g, rel_bias):
    B, T, H, Dh = q.shape
    nc = T // CHUNK
    f32 = jnp.float32
    q = rms_norm(q, q_g).astype(f32)
    k = rms_norm(k, k_g).astype(f32)
    v = v.astype(f32)
    qc = q.reshape(B, nc, CHUNK, H, Dh)

    def band(t):
        tc = t.reshape(B, nc, CHUNK, H, Dh)
        tp = jnp.pad(tc, ((0, 0), (LEFT_CHUNKS, 0), (0, 0), (0, 0), (0, 0)))
        return jnp.concatenate([tp[:, i:i + nc] for i in range(BAND_CHUNKS)], axis=2)

    kb, vb = band(k), band(v)
    scores = jnp.einsum('bcqhd,bckhd->bchqk', qc, kb) * (Dh ** -0.5)
    q_off = jnp.arange(CHUNK)
    k_off = jnp.arange(BAND_CHUNKS * CHUNK) - LEFT_CHUNKS * CHUNK
    rel = q_off[:, None] - k_off[None, :]
    idx = jnp.clip(rel, -REL_CLIP, REL_CLIP) + REL_CLIP
    bias = jnp.transpose(rel_bias.astype(f32)[idx], (2, 0, 1))
    key_chunk = jnp.arange(nc)[:, None] - LEFT_CHUNKS + jnp.arange(BAND_CHUNKS)[None, :]
    valid = jnp.repeat(key_chunk >= 0, CHUNK, axis=1)
    scores = jnp.where(valid[None, :, None, None, :], scores + bias[None, None], -1e30)
    p = jax.nn.softmax(scores, axis=-1)
    out = jnp.einsum('bchqk,bckhd->bcqhd', p, vb)
    return out.reshape(B, T, H * Dh)


def causal_depthwise_conv(x, w):
    C = x.shape[-1]
    return lax.conv_general_dilated(
        x.astype(jnp.float32), w.astype(jnp.float32)[:, None, :],
        window_strides=(1,), padding=[(CONV_K - 1, 0)],
        dimension_numbers=('NWC', 'WIO', 'NWC'), feature_group_count=C)


def gated_delta_rule(q, k, v, beta_logit, alpha_logit, a_log, dt_bias):
    B, T, H, Dk = q.shape
    Dv = v.shape[-1]
    nc = T // CHUNK
    f32 = jnp.float32
    q = l2norm(q.astype(f32)) * (Dk ** -0.5)
    k = l2norm(k.astype(f32))
    v = v.astype(f32)
    beta = jax.nn.sigmoid(beta_logit.astype(f32))
    g = -jnp.exp(a_log.astype(f32)) * jax.nn.softplus(alpha_logit.astype(f32) + dt_bias.astype(f32))

    def to_chunks(t):
        return t.reshape(B, nc, CHUNK, H, -1).transpose(0, 3, 1, 2, 4)

    qc, kc, vc = to_chunks(q), to_chunks(k), to_chunks(v)
    bc = beta.reshape(B, nc, CHUNK, H).transpose(0, 3, 1, 2)
    gc = jnp.cumsum(g.reshape(B, nc, CHUNK, H).transpose(0, 3, 1, 2), axis=-1)
    tri_incl = jnp.tril(jnp.ones((CHUNK, CHUNK), bool))
    tri_strict = jnp.tril(jnp.ones((CHUNK, CHUNK), bool), -1)
    gdiff = gc[..., :, None] - gc[..., None, :]
    decay = jnp.where(tri_incl, jnp.exp(jnp.where(tri_incl, gdiff, 0.0)), 0.0)
    kb = kc * bc[..., None]
    vb = vc * bc[..., None]
    L = jnp.where(tri_strict, jnp.einsum('bhnid,bhnjd->bhnij', kb, kc) * decay, 0.0)
    eye = jnp.eye(CHUNK, dtype=f32)
    rhs = jnp.concatenate([vb, kb * jnp.exp(gc)[..., None]], axis=-1)
    sol = lax.linalg.triangular_solve(eye + L, rhs, left_side=True, lower=True, unit_diagonal=True)
    u, w = sol[..., :Dv], sol[..., Dv:]
    intra = jnp.einsum('bhnid,bhnjd->bhnij', qc, kc) * decay
    q_dec = qc * jnp.exp(gc)[..., None]
    g_last = gc[..., -1]
    k_dec = kc * jnp.exp(g_last[..., None] - gc)[..., None]

    def step(S, inp):
        u_n, w_n, intra_n, qd_n, kd_n, gl_n = inp
        v_new = u_n - jnp.einsum('bhcd,bhde->bhce', w_n, S)
        o = jnp.einsum('bhcd,bhde->bhce', qd_n, S) + jnp.einsum('bhij,bhje->bhie', intra_n, v_new)
        S = S * jnp.exp(gl_n)[..., None, None] + jnp.einsum('bhcd,bhce->bhde', kd_n, v_new)
        return S, o

    xs = (jnp.moveaxis(u, 2, 0), jnp.moveaxis(w, 2, 0), jnp.moveaxis(intra, 2, 0),
          jnp.moveaxis(q_dec, 2, 0), jnp.moveaxis(k_dec, 2, 0), jnp.moveaxis(g_last, 2, 0))
    S0 = jnp.zeros((B, H, Dk, Dv), f32)
    _, o = lax.scan(step, S0, xs)
    return jnp.transpose(o, (1, 0, 3, 2, 4)).reshape(B, T, H, Dv)


def setup_inputs(seed: int = 0) -> dict:
    key = jax.random.key(seed)
    ks = jax.random.split(key, 17)
    f32 = jnp.float32

    def nrm(k, shape, scale):
        return jax.random.normal(k, shape, f32) * scale

    def gain(k, n):
        return 1.0 + 0.01 * jax.random.normal(k, (DEPTH, n), f32)

    x = jax.random.normal(ks[0], (BATCH, SEQ, D_MODEL), f32)
    dt = jnp.exp(jax.random.uniform(ks[9], (DEPTH, DN_HEADS), f32, math.log(1e-3), math.log(1e-1)))
    return {
        'x': x,
        'norm_mix_g': gain(ks[1], D_MODEL),
        'w_in': nrm(ks[2], (DEPTH, D_MODEL, IN_COLS), D_MODEL ** -0.5),
        'attn_q_norm_g': gain(ks[3], ATTN_HEAD_DIM),
        'attn_k_norm_g': gain(ks[4], ATTN_HEAD_DIM),
        'rel_bias': nrm(ks[5], (DEPTH, 2 * REL_CLIP + 1, ATTN_HEADS), 0.1),
        'attn_out_norm_g': gain(ks[6], ATTN_WIDTH),
        'conv_w': nrm(ks[7], (DEPTH, CONV_K, 3 * DN_WIDTH), CONV_K ** -0.5),
        'a_log': jnp.log(jax.random.uniform(ks[8], (DEPTH, DN_HEADS), f32, 1.0, 16.0)),
        'dt_bias': dt + jnp.log(-jnp.expm1(-dt)),
        'dn_out_norm_g': gain(ks[10], DN_HEAD_DIM),
        'w_out': nrm(ks[11], (DEPTH, D_MODEL, D_MODEL), D_MODEL ** -0.5),
        'norm_ffn_g': gain(ks[12], D_MODEL),
        'w_gate': nrm(ks[13], (DEPTH, D_MODEL, FFN_HIDDEN), D_MODEL ** -0.5),
        'w_up': nrm(ks[14], (DEPTH, D_MODEL, FFN_HIDDEN), D_MODEL ** -0.5),
        'w_down': nrm(ks[15], (DEPTH, FFN_HIDDEN, D_MODEL), FFN_HIDDEN ** -0.5),
    }


def reference(x, norm_mix_g, w_in, attn_q_norm_g, attn_k_norm_g, rel_bias, attn_out_norm_g,
              conv_w, a_log, dt_bias, dn_out_norm_g, w_out, norm_ffn_g, w_gate, w_up, w_down):
    B, T, _ = x.shape
    splits = [ATTN_WIDTH, 2 * ATTN_WIDTH, 3 * ATTN_WIDTH,
              3 * ATTN_WIDTH + 3 * DN_WIDTH, 3 * ATTN_WIDTH + 4 * DN_WIDTH,
              3 * ATTN_WIDTH + 4 * DN_WIDTH + DN_HEADS]
    for l in range(DEPTH):
        h = rms_norm(x, norm_mix_g[l])
        proj = jnp.einsum('bsd,de->bse', h, w_in[l])
        a_q, a_k, a_v, d_qkv, d_z, d_beta, d_alpha = jnp.split(proj, splits, axis=-1)
        a_out = chunked_band_attention(
            a_q.reshape(B, T, ATTN_HEADS, ATTN_HEAD_DIM),
            a_k.reshape(B, T, ATTN_HEADS, ATTN_HEAD_DIM),
            a_v.reshape(B, T, ATTN_HEADS, ATTN_HEAD_DIM),
            attn_q_norm_g[l], attn_k_norm_g[l], rel_bias[l])
        a_out = rms_norm(a_out, attn_out_norm_g[l]).astype(x.dtype)
        d_qkv = jax.nn.silu(causal_depthwise_conv(d_qkv, conv_w[l]))
        d_q, d_k, d_v = jnp.split(d_qkv, 3, axis=-1)
        o = gated_delta_rule(
            d_q.reshape(B, T, DN_HEADS, DN_HEAD_DIM),
            d_k.reshape(B, T, DN_HEADS, DN_HEAD_DIM),
            d_v.reshape(B, T, DN_HEADS, DN_HEAD_DIM),
            d_beta, d_alpha, a_log[l], dt_bias[l])
        z = d_z.reshape(B, T, DN_HEADS, DN_HEAD_DIM).astype(jnp.float32)
        o = rms_norm(o, dn_out_norm_g[l]) * jax.nn.silu(z)
        d_out = o.reshape(B, T, DN_WIDTH).astype(x.dtype)
        mix = jnp.concatenate([a_out, d_out], axis=-1)
        x = x + jnp.einsum('bse,ed->bsd', mix, w_out[l])
        h = rms_norm(x, norm_ffn_g[l])
        gate = jnp.einsum('bsd,df->bsf', h, w_gate[l])
        up = jnp.einsum('bsd,df->bsf', h, w_up[l])
        x = x + jnp.einsum('bsf,fd->bsd', jax.nn.silu(gate) * up, w_down[l])
    return x
```

```python
import functools

import jax
import jax.numpy as jnp
from jax import lax
from jax.experimental import pallas as pl
from jax.experimental.pallas import tpu as pltpu

F32 = jnp.float32
BF16 = jnp.bfloat16

D_MODEL = 1024
CHUNK = 64
ATTN_HEAD_DIM = 64
ATTN_HEADS = 8
ATTN_WIDTH = ATTN_HEADS * ATTN_HEAD_DIM
LEFT_CHUNKS = 8
REL_CLIP = 128
DN_HEAD_DIM = 128
DN_HEADS = 4
DN_WIDTH = DN_HEADS * DN_HEAD_DIM
CONV_K = 4
EPS = 1e-6
NEG_INF = -1e30

LANES = 128
SUBLANES = 8
VMEM_LIMIT = 56 * 1024 * 1024

PROJ_TM = 512
ATTN_TB = 512
PAIR = 2 * CHUNK
WIN = PAIR + LEFT_CHUNKS * CHUNK
DN_TB = 512
DN_NC = DN_TB // CHUNK
FFN_TM = 256
GATE_TB = 2048


def _resident(shape):
    return pl.BlockSpec(shape, lambda i: (0,) * len(shape), pipeline_mode=pl.Buffered(1))


def _dot(a, b):
    return jnp.dot(a, b, preferred_element_type=F32)


def _dot_nt(a, b):
    return lax.dot_general(a, b, (((1,), (1,)), ((), ())), preferred_element_type=F32)


def _silu(x):
    return x * jax.nn.sigmoid(x)


def _in_proj_kernel(x_ref, g_ref, wa_ref, wd_ref, wz_ref, wg_ref, a_ref, d_ref, z_ref, ba_ref):
    x = x_ref[...]
    ms = jnp.mean(x * x, axis=-1, keepdims=True)
    h = ((x * lax.rsqrt(ms + EPS)) * g_ref[...]).astype(BF16)
    a_ref[...] = _dot(h, wa_ref[...])
    d_ref[...] = _dot(h, wd_ref[...])
    z_ref[...] = _dot(h, wz_ref[...])
    ba_ref[...] = _dot(h, wg_ref[...])


def _in_proj(x2, g, wa, wd, wz, wg):
    t = x2.shape[0]
    row = lambda i: (i, 0)
    return pl.pallas_call(
        _in_proj_kernel,
        grid=(t // PROJ_TM,),
        in_specs=[
            pl.BlockSpec((PROJ_TM, D_MODEL), row),
            _resident((1, D_MODEL)),
            _resident(wa.shape), _resident(wd.shape), _resident(wz.shape), _resident(wg.shape),
        ],
        out_specs=[
            pl.BlockSpec((PROJ_TM, wa.shape[1]), row),
            pl.BlockSpec((PROJ_TM, wd.shape[1]), row),
            pl.BlockSpec((PROJ_TM, wz.shape[1]), row),
            pl.BlockSpec((PROJ_TM, wg.shape[1]), row),
        ],
        out_shape=[
            jax.ShapeDtypeStruct((t, wa.shape[1]), F32),
            jax.ShapeDtypeStruct((t, wd.shape[1]), F32),
            jax.ShapeDtypeStruct((t, wz.shape[1]), F32),
            jax.ShapeDtypeStruct((t, wg.shape[1]), F32),
        ],
        compiler_params=pltpu.CompilerParams(
            dimension_semantics=("arbitrary",), vmem_limit_bytes=VMEM_LIMIT),
        name="in_proj",
    )(x2, g, wa, wd, wz, wg)


def _head_mean_sq(x, bd):
    sq = x * x
    hi = sq.astype(BF16)
    lo = (sq - hi.astype(F32)).astype(BF16)
    return (_dot(hi, bd) + _dot(lo, bd)) * (1.0 / ATTN_HEAD_DIM)


def _attn_kernel(q_ref, k_ref, v_ref, qg_ref, kg_ref, e_ref, bd_ref, og_ref, o_ref,
                 bias_ref, qn_ref, kn_ref, vn_ref, ob_ref):
    i = pl.program_id(0)
    n_hp = ATTN_HEADS // 2

    @pl.when(i == 0)
    def _init():
        r = lax.broadcasted_iota(jnp.int32, (PAIR, WIN), 0)
        j = lax.broadcasted_iota(jnp.int32, (PAIR, WIN), 1)
        first_key = jnp.where(r < CHUNK, 0, CHUNK)
        valid = (j >= first_key) & (j < first_key + WIN - CHUNK)
        for h in range(ATTN_HEADS):
            eb = jnp.broadcast_to(e_ref[h:h + 1, :], (PAIR, WIN + PAIR))
            rolled = pltpu.roll(eb, 0, 1, stride=1, stride_axis=0)
            tab = jnp.where(valid, rolled[:, PAIR:], NEG_INF)
            bias_ref[h // 2, (h % 2) * PAIR:(h % 2 + 1) * PAIR, :] = tab
        kn_ref[0:ATTN_TB, :] = jnp.zeros((ATTN_TB, ATTN_WIDTH), BF16)
        vn_ref[0:ATTN_TB, :] = jnp.zeros((ATTN_TB, ATTN_WIDTH), BF16)

    @pl.when(i > 0)
    def _shift():
        kn_ref[0:ATTN_TB, :] = kn_ref[ATTN_TB:2 * ATTN_TB, :]
        vn_ref[0:ATTN_TB, :] = vn_ref[ATTN_TB:2 * ATTN_TB, :]

    bd = bd_ref[...]
    q = q_ref[...]
    qn = (q * lax.rsqrt(_head_mean_sq(q, bd) + EPS)) * qg_ref[...]
    qn_ref[...] = (qn * (ATTN_HEAD_DIM ** -0.5)).astype(BF16)
    k = k_ref[...]
    kn = (k * lax.rsqrt(_head_mean_sq(k, bd) + EPS)) * kg_ref[...]
    kn_ref[ATTN_TB:2 * ATTN_TB, :] = kn.astype(BF16)
    vn_ref[ATTN_TB:2 * ATTN_TB, :] = v_ref[...].astype(BF16)

    lane = lax.broadcasted_iota(jnp.int32, (PAIR, LANES), 1)
    first_half = lane < ATTN_HEAD_DIM

    def scores_block(first_block):
        def pair_body(p, carry):
            q0 = pl.multiple_of(p * PAIR, PAIR)
            for hp in range(n_hp):
                cols = slice(hp * LANES, (hp + 1) * LANES)
                q2 = qn_ref[pl.ds(q0, PAIR), cols]
                zero = jnp.zeros_like(q2)
                lhs = jnp.concatenate(
                    [jnp.where(first_half, q2, zero), jnp.where(first_half, zero, q2)], axis=0)
                k2 = kn_ref[pl.ds(q0, WIN), cols]
                v2 = vn_ref[pl.ds(q0, WIN), cols]
                s = _dot_nt(lhs, k2) + bias_ref[hp]
                if first_block:
                    jj = lax.broadcasted_iota(jnp.int32, (2 * PAIR, WIN), 1)
                    s = jnp.where(jj + q0 >= LEFT_CHUNKS * CHUNK, s, NEG_INF)
                m = jnp.max(s, axis=-1, keepdims=True)
                e = jnp.exp(s - m)
                l = jnp.sum(e, axis=-1, keepdims=True)
                pv = _dot(e.astype(BF16), v2) / l
                ob_ref[pl.ds(q0, PAIR), cols] = jnp.where(first_half, pv[:PAIR], pv[PAIR:])
            return carry
        lax.fori_loop(0, ATTN_TB // PAIR, pair_body, 0)

    @pl.when(i == 0)
    def _first():
        scores_block(True)

    @pl.when(i > 0)
    def _rest():
        scores_block(False)

    a = ob_ref[...]
    ms = jnp.mean(a * a, axis=-1, keepdims=True)
    o_ref[...] = (a * lax.rsqrt(ms + EPS)) * og_ref[...]


def _attention(a_qkv, qg, kg, e_tab, bd, og):
    t = a_qkv.shape[0]
    n_hp = ATTN_HEADS // 2
    return pl.pallas_call(
        _attn_kernel,
        grid=(t // ATTN_TB,),
        in_specs=[
            pl.BlockSpec((ATTN_TB, ATTN_WIDTH), lambda i: (i, 0)),
            pl.BlockSpec((ATTN_TB, ATTN_WIDTH), lambda i: (i, 1)),
            pl.BlockSpec((ATTN_TB, ATTN_WIDTH), lambda i: (i, 2)),
            _resident((1, ATTN_WIDTH)), _resident((1, ATTN_WIDTH)),
            _resident(e_tab.shape), _resident(bd.shape), _resident((1, ATTN_WIDTH)),
        ],
        out_specs=pl.BlockSpec((ATTN_TB, ATTN_WIDTH), lambda i: (i, 0)),
        out_shape=jax.ShapeDtypeStruct((t, ATTN_WIDTH), F32),
        scratch_shapes=[
            pltpu.VMEM((n_hp, 2 * PAIR, WIN), F32),
            pltpu.VMEM((ATTN_TB, ATTN_WIDTH), BF16),
            pltpu.VMEM((2 * ATTN_TB, ATTN_WIDTH), BF16),
            pltpu.VMEM((2 * ATTN_TB, ATTN_WIDTH), BF16),
            pltpu.VMEM((ATTN_TB, ATTN_WIDTH), F32),
        ],
        compiler_params=pltpu.CompilerParams(
            dimension_semantics=("arbitrary",), vmem_limit_bytes=VMEM_LIMIT),
        name="band_attn",
    )(a_qkv, a_qkv, a_qkv, qg, kg, e_tab, bd, og)


def _gates_kernel(ba_ref, alog_ref, dtb_ref, o_ref):
    x = ba_ref[...]
    beta = jax.nn.sigmoid(x)
    y = x + dtb_ref[...]
    softplus = jnp.maximum(y, 0.0) + jnp.log1p(jnp.exp(-jnp.abs(y)))
    g = -jnp.exp(alog_ref[...]) * softplus
    pos = lax.broadcasted_iota(jnp.int32, x.shape, 1) % CHUNK
    shift = 1
    while shift < CHUNK:
        g = g + jnp.where(pos >= shift, pltpu.roll(g, shift, 1), 0.0)
        shift *= 2
    row = lax.broadcasted_iota(jnp.int32, x.shape, 0)
    o_ref[...] = jnp.where(row < DN_HEADS, beta, g)


def _gates(ba_t, alog8, dtb8):
    t = ba_t.shape[1]
    return pl.pallas_call(
        _gates_kernel,
        grid=(t // GATE_TB,),
        in_specs=[
            pl.BlockSpec((2 * DN_HEADS, GATE_TB), lambda i: (0, i)),
            _resident((2 * DN_HEADS, 1)), _resident((2 * DN_HEADS, 1)),
        ],
        out_specs=pl.BlockSpec((2 * DN_HEADS, GATE_TB), lambda i: (0, i)),
        out_shape=jax.ShapeDtypeStruct((2 * DN_HEADS, t), F32),
        compiler_params=pltpu.CompilerParams(dimension_semantics=("arbitrary",)),
        name="dn_gates",
    )(ba_t, alog8, dtb8)


def _deltanet_kernel(d_ref, z_ref, gcol_ref, grow_ref, cw_ref, og_ref, o_ref,
                     buf_ref, qn_ref, kn_ref, vn_ref, u_ref, w_ref, qd_ref, kdt_ref,
                     intra_ref, s_ref, ob_ref):
    i = pl.program_id(0)
    hd = DN_HEAD_DIM
    pad = SUBLANES

    @pl.when(i == 0)
    def _init():
        buf_ref[0:pad, :] = jnp.zeros((pad, 3 * DN_WIDTH), F32)
        s_ref[...] = jnp.zeros(s_ref.shape, F32)

    @pl.when(i > 0)
    def _carry_tail():
        buf_ref[0:pad, :] = buf_ref[DN_TB:DN_TB + pad, :]

    buf_ref[pad:pad + DN_TB, :] = d_ref[...]

    for c in range(3 * DN_HEADS):
        cols = slice(c * hd, (c + 1) * hd)
        y = jnp.zeros((DN_TB, hd), F32)
        for tap in range(CONV_K):
            lo = pad - (CONV_K - 1) + tap
            y = y + cw_ref[tap:tap + 1, cols] * buf_ref[lo:lo + DN_TB, cols]
        y = _silu(y)
        part, h = divmod(c, DN_HEADS)
        hcols = slice(h * hd, (h + 1) * hd)
        if part == 2:
            vn_ref[:, hcols] = y
        else:
            yn = y * lax.rsqrt(jnp.sum(y * y, axis=-1, keepdims=True) + EPS)
            if part == 0:
                qn_ref[:, hcols] = yn * (hd ** -0.5)
            else:
                kn_ref[:, hcols] = yn

    ri = lax.broadcasted_iota(jnp.int32, (CHUNK, CHUNK), 0)
    ci = lax.broadcasted_iota(jnp.int32, (CHUNK, CHUNK), 1)
    tri_incl = ri >= ci
    tri_strict = ri > ci
    eye = (ri == ci).astype(F32)

    def local_body(n, carry):
        r0 = pl.multiple_of(n * CHUNK, CHUNK)
        gcol = gcol_ref[pl.ds(r0, CHUNK), :]
        grow = grow_ref[n]
        for h in range(DN_HEADS):
            hcols = slice(h * hd, (h + 1) * hd)
            k = kn_ref[pl.ds(r0, CHUNK), hcols]
            q = qn_ref[pl.ds(r0, CHUNK), hcols]
            v = vn_ref[pl.ds(r0, CHUNK), hcols]
            beta = gcol[:, h:h + 1]
            gc = gcol[:, DN_HEADS + h:DN_HEADS + h + 1]
            gcr = grow[DN_HEADS + h:DN_HEADS + h + 1, :]
            gdiff = gc - gcr
            decay = jnp.where(tri_incl, jnp.exp(jnp.where(tri_incl, gdiff, 0.0)), 0.0)
            kb = k * beta
            vb = v * beta
            kbf = k.astype(BF16)
            lmat = jnp.where(tri_strict, _dot_nt(kb.astype(BF16), kbf) * decay, 0.0)
            x = eye - lmat
            pw = lmat.astype(BF16)
            for _ in range(5):
                pw32 = _dot(pw, pw)
                pw = pw32.astype(BF16)
                x = x + _dot(x.astype(BF16), pw)
            eg = jnp.exp(gc)
            rhs = jnp.concatenate([vb, kb * eg], axis=-1).astype(BF16)
            sol = _dot(x.astype(BF16), rhs)
            u_ref[n, h] = sol[:, :hd]
            w_ref[n, h] = sol[:, hd:]
            intra_ref[n, h] = _dot_nt(q.astype(BF16), kbf) * decay
            qd_ref[n, h] = q * eg
            g_last = gcr[:, CHUNK - 1:CHUNK]
            kd = k * jnp.exp(g_last - gc)
            kdt_ref[n, h] = kd.T
        return carry
    lax.fori_loop(0, DN_NC, local_body, 0, unroll=2)

    def scan_body(n, carry):
        r0 = pl.multiple_of(n * CHUNK, CHUNK)
        grow = grow_ref[n]
        for h in range(DN_HEADS):
            hcols = slice(h * hd, (h + 1) * hd)
            s = s_ref[h]
            sb = s.astype(BF16)
            v_new = u_ref[n, h] - _dot(w_ref[n, h].astype(BF16), sb)
            vnb = v_new.astype(BF16)
            o = _dot(qd_ref[n, h].astype(BF16), sb) + _dot(intra_ref[n, h].astype(BF16), vnb)
            g_last = grow[DN_HEADS + h:DN_HEADS + h + 1, CHUNK - 1:CHUNK]
            s_ref[h] = s * jnp.exp(g_last) + _dot(kdt_ref[n, h].astype(BF16), vnb)
            ob_ref[pl.ds(r0, CHUNK), hcols] = o
        return carry
    lax.fori_loop(0, DN_NC, scan_body, 0)

    for h in range(DN_HEADS):
        hcols = slice(h * hd, (h + 1) * hd)
        o = ob_ref[:, hcols]
        ms = jnp.mean(o * o, axis=-1, keepdims=True)
        o_ref[:, hcols] = ((o * lax.rsqrt(ms + EPS)) * og_ref[:, hcols]) * _silu(z_ref[:, hcols])


def _deltanet(d_qkv, z, gcol, grow, conv_w, og):
    t = d_qkv.shape[0]
    hd = DN_HEAD_DIM
    return pl.pallas_call(
        _deltanet_kernel,
        grid=(t // DN_TB,),
        in_specs=[
            pl.BlockSpec((DN_TB, 3 * DN_WIDTH), lambda i: (i, 0)),
            pl.BlockSpec((DN_TB, DN_WIDTH), lambda i: (i, 0)),
            pl.BlockSpec((DN_TB, 2 * DN_HEADS), lambda i: (i, 0)),
            pl.BlockSpec((DN_NC, 2 * DN_HEADS, CHUNK), lambda i: (i, 0, 0)),
            _resident((CONV_K, 3 * DN_WIDTH)), _resident((1, DN_WIDTH)),
        ],
        out_specs=pl.BlockSpec((DN_TB, DN_WIDTH), lambda i: (i, 0)),
        out_shape=jax.ShapeDtypeStruct((t, DN_WIDTH), F32),
        scratch_shapes=[
            pltpu.VMEM((DN_TB + 2 * SUBLANES, 3 * DN_WIDTH), F32),
            pltpu.VMEM((DN_TB, DN_WIDTH), F32),
            pltpu.VMEM((DN_TB, DN_WIDTH), F32),
            pltpu.VMEM((DN_TB, DN_WIDTH), F32),
            pltpu.VMEM((DN_NC, DN_HEADS, CHUNK, hd), F32),
            pltpu.VMEM((DN_NC, DN_HEADS, CHUNK, hd), F32),
            pltpu.VMEM((DN_NC, DN_HEADS, CHUNK, hd), F32),
            pltpu.VMEM((DN_NC, DN_HEADS, hd, CHUNK), F32),
            pltpu.VMEM((DN_NC, DN_HEADS, CHUNK, CHUNK), F32),
            pltpu.VMEM((DN_HEADS, hd, hd), F32),
            pltpu.VMEM((DN_TB, DN_WIDTH), F32),
        ],
        compiler_params=pltpu.CompilerParams(
            dimension_semantics=("arbitrary",), vmem_limit_bytes=VMEM_LIMIT),
        name="deltanet",
    )(d_qkv, z, gcol, grow, conv_w, og)


def _out_ffn_kernel(x_ref, a_ref, d_ref, wo_ref, g_ref, wg_ref, wu_ref, wd_ref, o_ref):
    wo_a = wo_ref[0:ATTN_WIDTH, :]
    wo_d = wo_ref[ATTN_WIDTH:D_MODEL, :]
    x1 = x_ref[...] + _dot(a_ref[...].astype(BF16), wo_a) + _dot(d_ref[...].astype(BF16), wo_d)
    ms = jnp.mean(x1 * x1, axis=-1, keepdims=True)
    h = ((x1 * lax.rsqrt(ms + EPS)) * g_ref[...]).astype(BF16)
    gate = _dot(h, wg_ref[...])
    up = _dot(h, wu_ref[...])
    y = (_silu(gate) * up).astype(BF16)
    o_ref[...] = x1 + _dot(y, wd_ref[...])


def _out_ffn(x2, a_out, d_out, wo, g, wg, wu, wd):
    t = x2.shape[0]
    row = lambda i: (i, 0)
    return pl.pallas_call(
        _out_ffn_kernel,
        grid=(t // FFN_TM,),
        in_specs=[
            pl.BlockSpec((FFN_TM, D_MODEL), row),
            pl.BlockSpec((FFN_TM, ATTN_WIDTH), row),
            pl.BlockSpec((FFN_TM, DN_WIDTH), row),
            _resident(wo.shape), _resident((1, D_MODEL)),
            _resident(wg.shape), _resident(wu.shape), _resident(wd.shape),
        ],
        out_specs=pl.BlockSpec((FFN_TM, D_MODEL), row),
        out_shape=jax.ShapeDtypeStruct((t, D_MODEL), F32),
        compiler_params=pltpu.CompilerParams(
            dimension_semantics=("arbitrary",), vmem_limit_bytes=VMEM_LIMIT),
        name="out_ffn",
    )(x2, a_out, d_out, wo, g, wg, wu, wd)


def _layer(x2, norm_mix_g, w_in, attn_q_norm_g, attn_k_norm_g, rel_bias, attn_out_norm_g,
           conv_w, a_log, dt_bias, dn_out_norm_g, w_out, norm_ffn_g, w_gate, w_up, w_down):
    t = x2.shape[0]
    na = 3 * ATTN_WIDTH
    nd = 3 * DN_WIDTH
    wb = w_in.astype(BF16)
    wa = wb[:, :na]
    wd = wb[:, na:na + nd]
    wz = wb[:, na + nd:na + nd + DN_WIDTH]
    wg = jnp.pad(wb[:, na + nd + DN_WIDTH:], ((0, 0), (0, LANES - 2 * DN_HEADS)))
    a_qkv, d_qkv, z, ba = _in_proj(x2, norm_mix_g[None, :], wa, wd, wz, wg)

    qg = jnp.tile(attn_q_norm_g, ATTN_HEADS)[None, :]
    kg = jnp.tile(attn_k_norm_g, ATTN_HEADS)[None, :]
    rb_t = rel_bias.T
    far = jnp.broadcast_to(rb_t[:, 2 * REL_CLIP:], (ATTN_HEADS, WIN + PAIR - 2 * REL_CLIP))
    e_tab = jnp.concatenate([far, rb_t[:, 2 * REL_CLIP:0:-1]], axis=1)
    hid = jnp.arange(ATTN_WIDTH) // ATTN_HEAD_DIM
    bd = (hid[:, None] == hid[None, :]).astype(BF16)
    a_out = _attention(a_qkv, qg, kg, e_tab, bd, attn_out_norm_g[None, :])

    zeros4 = jnp.zeros((DN_HEADS,), F32)
    alog8 = jnp.concatenate([zeros4, a_log])[:, None]
    dtb8 = jnp.concatenate([zeros4, dt_bias])[:, None]
    gates_t = _gates(ba[:, :2 * DN_HEADS].T, alog8, dtb8)
    gcol = gates_t.T
    grow = gates_t.reshape(2 * DN_HEADS, t // CHUNK, CHUNK).transpose(1, 0, 2)
    og = jnp.tile(dn_out_norm_g, DN_HEADS)[None, :]
    d_out = _deltanet(d_qkv, z, gcol, grow, conv_w, og)

    return _out_ffn(x2, a_out, d_out, w_out.astype(BF16), norm_ffn_g[None, :],
                    w_gate.astype(BF16), w_up.astype(BF16), w_down.astype(BF16))


def kernel(x, norm_mix_g, w_in, attn_q_norm_g, attn_k_norm_g, rel_bias, attn_out_norm_g, conv_w, a_log, dt_bias, dn_out_norm_g, w_out, norm_ffn_g, w_gate, w_up, w_down):
    b, t, d = x.shape
    outs = []
    for bi in range(b):
        x2 = x[bi]
        for l in range(w_in.shape[0]):
            x2 = _layer(x2, norm_mix_g[l], w_in[l], attn_q_norm_g[l], attn_k_norm_g[l], rel_bias[l],
                        attn_out_norm_g[l], conv_w[l], a_log[l], dt_bias[l], dn_out_norm_g[l],
                        w_out[l], norm_ffn_g[l], w_gate[l], w_up[l], w_down[l])
        outs.append(x2)
    return jnp.stack(outs, axis=0) if b > 1 else outs[0][None]
```

```python
import functools

import jax
import jax.numpy as jnp
from jax import lax
from jax.experimental import pallas as pl
from jax.experimental.pallas import tpu as pltpu

F32 = jnp.float32
BF16 = jnp.bfloat16

D_MODEL = 1024
CHUNK = 64
ATTN_HEAD_DIM = 64
ATTN_HEADS = 8
ATTN_WIDTH = ATTN_HEADS * ATTN_HEAD_DIM
LEFT_CHUNKS = 8
REL_CLIP = 128
DN_HEAD_DIM = 128
DN_HEADS = 4
DN_WIDTH = DN_HEADS * DN_HEAD_DIM
CONV_K = 4
EPS = 1e-6
NEG_INF = -1e30

LANES = 128
SUBLANES = 8
VMEM_LIMIT = 56 * 1024 * 1024

PROJ_TM = 512
ATTN_TB = 512
PAIR = 2 * CHUNK
WIN = PAIR + LEFT_CHUNKS * CHUNK
DN_TB = 512
DN_NC = DN_TB // CHUNK
FFN_TM = 256
GATE_TB = 2048


def _resident(shape):
    return pl.BlockSpec(shape, lambda i: (0,) * len(shape), pipeline_mode=pl.Buffered(1))


def _dot(a, b):
    return jnp.dot(a, b, preferred_element_type=F32)


def _dot_nt(a, b):
    return lax.dot_general(a, b, (((1,), (1,)), ((), ())), preferred_element_type=F32)


def _bdot(a, b):
    return lax.dot_general(a, b, (((2,), (1,)), ((0,), (0,))), preferred_element_type=F32)


def _bdot_nt(a, b):
    return lax.dot_general(a, b, (((2,), (2,)), ((0,), (0,))), preferred_element_type=F32)


def _silu(x):
    return x * jax.nn.sigmoid(x)


def _in_proj_kernel(x_ref, g_ref, wa_ref, wd_ref, wz_ref, wg_ref, a_ref, d_ref, z_ref, ba_ref):
    x = x_ref[...]
    ms = jnp.mean(x * x, axis=-1, keepdims=True)
    h = ((x * lax.rsqrt(ms + EPS)) * g_ref[...]).astype(BF16)
    a_ref[...] = _dot(h, wa_ref[...])
    d_ref[...] = _dot(h, wd_ref[...])
    z_ref[...] = _dot(h, wz_ref[...])
    ba_ref[...] = _dot(h, wg_ref[...])


def _in_proj(x2, g, wa, wd, wz, wg):
    t = x2.shape[0]
    row = lambda i: (i, 0)
    return pl.pallas_call(
        _in_proj_kernel,
        grid=(t // PROJ_TM,),
        in_specs=[
            pl.BlockSpec((PROJ_TM, D_MODEL), row),
            _resident((1, D_MODEL)),
            _resident(wa.shape), _resident(wd.shape), _resident(wz.shape), _resident(wg.shape),
        ],
        out_specs=[
            pl.BlockSpec((PROJ_TM, wa.shape[1]), row),
            pl.BlockSpec((PROJ_TM, wd.shape[1]), row),
            pl.BlockSpec((PROJ_TM, wz.shape[1]), row),
            pl.BlockSpec((PROJ_TM, wg.shape[1]), row),
        ],
        out_shape=[
            jax.ShapeDtypeStruct((t, wa.shape[1]), F32),
            jax.ShapeDtypeStruct((t, wd.shape[1]), F32),
            jax.ShapeDtypeStruct((t, wz.shape[1]), F32),
            jax.ShapeDtypeStruct((t, wg.shape[1]), F32),
        ],
        compiler_params=pltpu.CompilerParams(
            dimension_semantics=("arbitrary",), vmem_limit_bytes=VMEM_LIMIT),
        name="in_proj",
    )(x2, g, wa, wd, wz, wg)


def _head_mean_sq(x, bd):
    sq = x * x
    hi = sq.astype(BF16)
    lo = (sq - hi.astype(F32)).astype(BF16)
    return (_dot(hi, bd) + _dot(lo, bd)) * (1.0 / ATTN_HEAD_DIM)


def _attn_kernel(q_ref, k_ref, v_ref, qg_ref, kg_ref, e_ref, bd_ref, og_ref, o_ref,
                 bias_ref, qn_ref, kn_ref, vn_ref, ob_ref):
    i = pl.program_id(0)
    n_hp = ATTN_HEADS // 2

    @pl.when(i == 0)
    def _init():
        r = lax.broadcasted_iota(jnp.int32, (PAIR, WIN), 0)
        j = lax.broadcasted_iota(jnp.int32, (PAIR, WIN), 1)
        first_key = jnp.where(r < CHUNK, 0, CHUNK)
        valid = (j >= first_key) & (j < first_key + WIN - CHUNK)
        for h in range(ATTN_HEADS):
            eb = jnp.broadcast_to(e_ref[h:h + 1, :], (PAIR, WIN + PAIR))
            rolled = pltpu.roll(eb, 0, 1, stride=1, stride_axis=0)
            tab = jnp.where(valid, rolled[:, PAIR:], NEG_INF)
            bias_ref[h // 2, (h % 2) * PAIR:(h % 2 + 1) * PAIR, :] = tab
        kn_ref[0:ATTN_TB, :] = jnp.zeros((ATTN_TB, ATTN_WIDTH), BF16)
        vn_ref[0:ATTN_TB, :] = jnp.zeros((ATTN_TB, ATTN_WIDTH), BF16)

    @pl.when(i > 0)
    def _shift():
        kn_ref[0:ATTN_TB, :] = kn_ref[ATTN_TB:2 * ATTN_TB, :]
        vn_ref[0:ATTN_TB, :] = vn_ref[ATTN_TB:2 * ATTN_TB, :]

    bd = bd_ref[...]
    q = q_ref[...]
    qn = (q * lax.rsqrt(_head_mean_sq(q, bd) + EPS)) * qg_ref[...]
    qn_ref[...] = (qn * (ATTN_HEAD_DIM ** -0.5)).astype(BF16)
    k = k_ref[...]
    kn = (k * lax.rsqrt(_head_mean_sq(k, bd) + EPS)) * kg_ref[...]
    kn_ref[ATTN_TB:2 * ATTN_TB, :] = kn.astype(BF16)
    vn_ref[ATTN_TB:2 * ATTN_TB, :] = v_ref[...].astype(BF16)

    lane = lax.broadcasted_iota(jnp.int32, (PAIR, LANES), 1)
    first_half = lane < ATTN_HEAD_DIM

    def scores_block(first_block):
        def pair_body(p, carry):
            q0 = pl.multiple_of(p * PAIR, PAIR)
            for hp in range(n_hp):
                cols = slice(hp * LANES, (hp + 1) * LANES)
                q2 = qn_ref[pl.ds(q0, PAIR), cols]
                zero = jnp.zeros_like(q2)
                lhs = jnp.concatenate(
                    [jnp.where(first_half, q2, zero), jnp.where(first_half, zero, q2)], axis=0)
                k2 = kn_ref[pl.ds(q0, WIN), cols]
                v2 = vn_ref[pl.ds(q0, WIN), cols]
                s = _dot_nt(lhs, k2) + bias_ref[hp]
                if first_block:
                    jj = lax.broadcasted_iota(jnp.int32, (2 * PAIR, WIN), 1)
                    s = jnp.where(jj + q0 >= LEFT_CHUNKS * CHUNK, s, NEG_INF)
                m = jnp.max(s, axis=-1, keepdims=True)
                e = jnp.exp(s - m)
                l = jnp.sum(e, axis=-1, keepdims=True)
                pv = _dot(e.astype(BF16), v2) / l
                ob_ref[pl.ds(q0, PAIR), cols] = jnp.where(first_half, pv[:PAIR], pv[PAIR:])
            return carry
        lax.fori_loop(0, ATTN_TB // PAIR, pair_body, 0)

    @pl.when(i == 0)
    def _first():
        scores_block(True)

    @pl.when(i > 0)
    def _rest():
        scores_block(False)

    a = ob_ref[...]
    ms = jnp.mean(a * a, axis=-1, keepdims=True)
    o_ref[...] = (a * lax.rsqrt(ms + EPS)) * og_ref[...]


def _attention(a_qkv, qg, kg, e_tab, bd, og):
    t = a_qkv.shape[0]
    n_hp = ATTN_HEADS // 2
    return pl.pallas_call(
        _attn_kernel,
        grid=(t // ATTN_TB,),
        in_specs=[
            pl.BlockSpec((ATTN_TB, ATTN_WIDTH), lambda i: (i, 0)),
            pl.BlockSpec((ATTN_TB, ATTN_WIDTH), lambda i: (i, 1)),
            pl.BlockSpec((ATTN_TB, ATTN_WIDTH), lambda i: (i, 2)),
            _resident((1, ATTN_WIDTH)), _resident((1, ATTN_WIDTH)),
            _resident(e_tab.shape), _resident(bd.shape), _resident((1, ATTN_WIDTH)),
        ],
        out_specs=pl.BlockSpec((ATTN_TB, ATTN_WIDTH), lambda i: (i, 0)),
        out_shape=jax.ShapeDtypeStruct((t, ATTN_WIDTH), F32),
        scratch_shapes=[
            pltpu.VMEM((n_hp, 2 * PAIR, WIN), F32),
            pltpu.VMEM((ATTN_TB, ATTN_WIDTH), BF16),
            pltpu.VMEM((2 * ATTN_TB, ATTN_WIDTH), BF16),
            pltpu.VMEM((2 * ATTN_TB, ATTN_WIDTH), BF16),
            pltpu.VMEM((ATTN_TB, ATTN_WIDTH), F32),
        ],
        compiler_params=pltpu.CompilerParams(
            dimension_semantics=("arbitrary",), vmem_limit_bytes=VMEM_LIMIT),
        name="band_attn",
    )(a_qkv, a_qkv, a_qkv, qg, kg, e_tab, bd, og)


def _gates_kernel(ba_ref, alog_ref, dtb_ref, o_ref):
    x = ba_ref[...]
    beta = jax.nn.sigmoid(x)
    y = x + dtb_ref[...]
    softplus = jnp.maximum(y, 0.0) + jnp.log1p(jnp.exp(-jnp.abs(y)))
    g = -jnp.exp(alog_ref[...]) * softplus
    pos = lax.broadcasted_iota(jnp.int32, x.shape, 1) % CHUNK
    shift = 1
    while shift < CHUNK:
        g = g + jnp.where(pos >= shift, pltpu.roll(g, shift, 1), 0.0)
        shift *= 2
    row = lax.broadcasted_iota(jnp.int32, x.shape, 0)
    o_ref[...] = jnp.where(row < DN_HEADS, beta, g)


def _gates(ba_t, alog8, dtb8):
    t = ba_t.shape[1]
    return pl.pallas_call(
        _gates_kernel,
        grid=(t // GATE_TB,),
        in_specs=[
            pl.BlockSpec((2 * DN_HEADS, GATE_TB), lambda i: (0, i)),
            _resident((2 * DN_HEADS, 1)), _resident((2 * DN_HEADS, 1)),
        ],
        out_specs=pl.BlockSpec((2 * DN_HEADS, GATE_TB), lambda i: (0, i)),
        out_shape=jax.ShapeDtypeStruct((2 * DN_HEADS, t), F32),
        compiler_params=pltpu.CompilerParams(dimension_semantics=("arbitrary",)),
        name="dn_gates",
    )(ba_t, alog8, dtb8)


def _deltanet_kernel(d_ref, z_ref, gcol_ref, grow_ref, cw_ref, og_ref, o_ref,
                     buf_ref, qn_ref, kn_ref, vn_ref, qs_ref, o0_ref, m_ref, bb_ref, egl_ref,
                     s_ref, ob_ref):
    i = pl.program_id(0)
    hd = DN_HEAD_DIM
    pad = SUBLANES

    @pl.when(i == 0)
    def _init():
        buf_ref[0:pad, :] = jnp.zeros((pad, 3 * DN_WIDTH), F32)
        s_ref[...] = jnp.zeros(s_ref.shape, F32)

    @pl.when(i > 0)
    def _carry_tail():
        buf_ref[0:pad, :] = buf_ref[DN_TB:DN_TB + pad, :]

    buf_ref[pad:pad + DN_TB, :] = d_ref[...]

    for c in range(3 * DN_HEADS):
        cols = slice(c * hd, (c + 1) * hd)
        y = jnp.zeros((DN_TB, hd), F32)
        for tap in range(CONV_K):
            lo = pad - (CONV_K - 1) + tap
            y = y + cw_ref[tap:tap + 1, cols] * buf_ref[lo:lo + DN_TB, cols]
        y = _silu(y)
        part, h = divmod(c, DN_HEADS)
        hcols = slice(h * hd, (h + 1) * hd)
        if part == 2:
            vn_ref[:, hcols] = y
        else:
            yn = y * lax.rsqrt(jnp.sum(y * y, axis=-1, keepdims=True) + EPS)
            if part == 0:
                qn_ref[:, hcols] = yn * (hd ** -0.5)
            else:
                kn_ref[:, hcols] = yn

    def per_head(x, width, off=0):
        x3 = x.reshape(DN_NC, CHUNK, x.shape[-1])
        return jnp.concatenate(
            [x3[:, :, off + h * width:off + (h + 1) * width] for h in range(DN_HEADS)], axis=0)

    k = per_head(kn_ref[...], hd)
    q = per_head(qn_ref[...], hd)
    v = per_head(vn_ref[...], hd)
    gcol = gcol_ref[...]
    beta = per_head(gcol, 1)
    gc = per_head(gcol, 1, DN_HEADS)
    grow = grow_ref[...]
    gcr = jnp.concatenate(
        [grow[:, DN_HEADS + h:DN_HEADS + h + 1, :] for h in range(DN_HEADS)], axis=0)
    g_last = gcr[:, :, CHUNK - 1:CHUNK]

    ri = lax.broadcasted_iota(jnp.int32, (1, CHUNK, CHUNK), 1)
    ci = lax.broadcasted_iota(jnp.int32, (1, CHUNK, CHUNK), 2)
    tri_incl = ri >= ci
    decay = jnp.where(tri_incl, jnp.exp(jnp.where(tri_incl, gc - gcr, 0.0)), 0.0)
    kb = k * beta
    kbf = k.astype(BF16)
    kkqk = _bdot_nt(jnp.concatenate([kb, q], axis=1).astype(BF16), kbf)
    lmat = jnp.where(ri > ci, kkqk[:, :CHUNK] * decay, 0.0)
    intra = kkqk[:, CHUNK:] * decay
    x = jnp.where(ri == ci, 1.0, 0.0) - lmat
    lb = lmat.astype(BF16)
    pw = _bdot(lb, lb)
    n_levels = 5
    for lvl in range(n_levels):
        pwb = pw.astype(BF16)
        if lvl + 1 < n_levels:
            r = _bdot(jnp.concatenate([pwb, x.astype(BF16)], axis=1), pwb)
            pw = r[:, :CHUNK]
            x = x + r[:, CHUNK:]
        else:
            x = x + _bdot(x.astype(BF16), pwb)
    eg = jnp.exp(gc)
    rhs = jnp.concatenate([v * beta, kb * eg], axis=-1).astype(BF16)
    sol = _bdot(x.astype(BF16), rhs)
    solb = sol.astype(BF16)
    iuw = _bdot(intra.astype(BF16), solb)
    kd = k * jnp.exp(g_last - gc)
    kuw = _bdot(jnp.swapaxes(kd, 1, 2).astype(BF16), solb)
    qs_ref[...] = (q * eg - iuw[:, :, hd:]).astype(BF16)
    o0_ref[...] = iuw[:, :, :hd]
    m_ref[...] = kuw[:, :, hd:].astype(BF16)
    bb_ref[...] = kuw[:, :, :hd]
    egl_ref[...] = jnp.broadcast_to(jnp.exp(g_last), (DN_NC * DN_HEADS, 1, hd))

    def scan_body(n, carry):
        r0 = pl.multiple_of(n * CHUNK, CHUNK)
        for h in range(DN_HEADS):
            b = h * DN_NC + n
            s = s_ref[h]
            sb = s.astype(BF16)
            ob_ref[pl.ds(r0, CHUNK), h * hd:(h + 1) * hd] = _dot(qs_ref[b], sb) + o0_ref[b]
            s_ref[h] = (s * egl_ref[b] - _dot(m_ref[b], sb)) + bb_ref[b]
        return carry
    lax.fori_loop(0, DN_NC, scan_body, 0)

    for h in range(DN_HEADS):
        hcols = slice(h * hd, (h + 1) * hd)
        o = ob_ref[:, hcols]
        ms = jnp.mean(o * o, axis=-1, keepdims=True)
        o_ref[:, hcols] = ((o * lax.rsqrt(ms + EPS)) * og_ref[:, hcols]) * _silu(z_ref[:, hcols])


def _deltanet(d_qkv, z, gcol, grow, conv_w, og):
    t = d_qkv.shape[0]
    hd = DN_HEAD_DIM
    nb = DN_NC * DN_HEADS
    return pl.pallas_call(
        _deltanet_kernel,
        grid=(t // DN_TB,),
        in_specs=[
            pl.BlockSpec((DN_TB, 3 * DN_WIDTH), lambda i: (i, 0)),
            pl.BlockSpec((DN_TB, DN_WIDTH), lambda i: (i, 0)),
            pl.BlockSpec((DN_TB, 2 * DN_HEADS), lambda i: (i, 0)),
            pl.BlockSpec((DN_NC, 2 * DN_HEADS, CHUNK), lambda i: (i, 0, 0)),
            _resident((CONV_K, 3 * DN_WIDTH)), _resident((1, DN_WIDTH)),
        ],
        out_specs=pl.BlockSpec((DN_TB, DN_WIDTH), lambda i: (i, 0)),
        out_shape=jax.ShapeDtypeStruct((t, DN_WIDTH), F32),
        scratch_shapes=[
            pltpu.VMEM((DN_TB + 2 * SUBLANES, 3 * DN_WIDTH), F32),
            pltpu.VMEM((DN_TB, DN_WIDTH), F32),
            pltpu.VMEM((DN_TB, DN_WIDTH), F32),
            pltpu.VMEM((DN_TB, DN_WIDTH), F32),
            pltpu.VMEM((nb, CHUNK, hd), BF16),
            pltpu.VMEM((nb, CHUNK, hd), F32),
            pltpu.VMEM((nb, hd, hd), BF16),
            pltpu.VMEM((nb, hd, hd), F32),
            pltpu.VMEM((nb, 1, hd), F32),
            pltpu.VMEM((DN_HEADS, hd, hd), F32),
            pltpu.VMEM((DN_TB, DN_WIDTH), F32),
        ],
        compiler_params=pltpu.CompilerParams(
            dimension_semantics=("arbitrary",), vmem_limit_bytes=VMEM_LIMIT),
        name="deltanet",
    )(d_qkv, z, gcol, grow, conv_w, og)


def _out_ffn_kernel(x_ref, a_ref, d_ref, wo_ref, g_ref, wg_ref, wu_ref, wd_ref, o_ref):
    wo_a = wo_ref[0:ATTN_WIDTH, :]
    wo_d = wo_ref[ATTN_WIDTH:D_MODEL, :]
    x1 = x_ref[...] + _dot(a_ref[...].astype(BF16), wo_a) + _dot(d_ref[...].astype(BF16), wo_d)
    ms = jnp.mean(x1 * x1, axis=-1, keepdims=True)
    h = ((x1 * lax.rsqrt(ms + EPS)) * g_ref[...]).astype(BF16)
    gate = _dot(h, wg_ref[...])
    up = _dot(h, wu_ref[...])
    y = (_silu(gate) * up).astype(BF16)
    o_ref[...] = x1 + _dot(y, wd_ref[...])


def _out_ffn(x2, a_out, d_out, wo, g, wg, wu, wd):
    t = x2.shape[0]
    row = lambda i: (i, 0)
    return pl.pallas_call(
        _out_ffn_kernel,
        grid=(t // FFN_TM,),
        in_specs=[
            pl.BlockSpec((FFN_TM, D_MODEL), row),
            pl.BlockSpec((FFN_TM, ATTN_WIDTH), row),
            pl.BlockSpec((FFN_TM, DN_WIDTH), row),
            _resident(wo.shape), _resident((1, D_MODEL)),
            _resident(wg.shape), _resident(wu.shape), _resident(wd.shape),
        ],
        out_specs=pl.BlockSpec((FFN_TM, D_MODEL), row),
        out_shape=jax.ShapeDtypeStruct((t, D_MODEL), F32),
        compiler_params=pltpu.CompilerParams(
            dimension_semantics=("arbitrary",), vmem_limit_bytes=VMEM_LIMIT),
        name="out_ffn",
    )(x2, a_out, d_out, wo, g, wg, wu, wd)


def _layer(x2, norm_mix_g, w_in, attn_q_norm_g, attn_k_norm_g, rel_bias, attn_out_norm_g,
           conv_w, a_log, dt_bias, dn_out_norm_g, w_out, norm_ffn_g, w_gate, w_up, w_down):
    t = x2.shape[0]
    na = 3 * ATTN_WIDTH
    nd = 3 * DN_WIDTH
    wb = w_in.astype(BF16)
    wa = wb[:, :na]
    wd = wb[:, na:na + nd]
    wz = wb[:, na + nd:na + nd + DN_WIDTH]
    wg = jnp.pad(wb[:, na + nd + DN_WIDTH:], ((0, 0), (0, LANES - 2 * DN_HEADS)))
    a_qkv, d_qkv, z, ba = _in_proj(x2, norm_mix_g[None, :], wa, wd, wz, wg)

    qg = jnp.tile(attn_q_norm_g, ATTN_HEADS)[None, :]
    kg = jnp.tile(attn_k_norm_g, ATTN_HEADS)[None, :]
    rb_t = rel_bias.T
    far = jnp.broadcast_to(rb_t[:, 2 * REL_CLIP:], (ATTN_HEADS, WIN + PAIR - 2 * REL_CLIP))
    e_tab = jnp.concatenate([far, rb_t[:, 2 * REL_CLIP:0:-1]], axis=1)
    hid = jnp.arange(ATTN_WIDTH) // ATTN_HEAD_DIM
    bd = (hid[:, None] == hid[None, :]).astype(BF16)
    a_out = _attention(a_qkv, qg, kg, e_tab, bd, attn_out_norm_g[None, :])

    zeros4 = jnp.zeros((DN_HEADS,), F32)
    alog8 = jnp.concatenate([zeros4, a_log])[:, None]
    dtb8 = jnp.concatenate([zeros4, dt_bias])[:, None]
    gates_t = _gates(ba[:, :2 * DN_HEADS].T, alog8, dtb8)
    gcol = gates_t.T
    grow = gates_t.reshape(2 * DN_HEADS, t // CHUNK, CHUNK).transpose(1, 0, 2)
    og = jnp.tile(dn_out_norm_g, DN_HEADS)[None, :]
    d_out = _deltanet(d_qkv, z, gcol, grow, conv_w, og)

    return _out_ffn(x2, a_out, d_out, w_out.astype(BF16), norm_ffn_g[None, :],
                    w_gate.astype(BF16), w_up.astype(BF16), w_down.astype(BF16))


def kernel(x, norm_mix_g, w_in, attn_q_norm_g, attn_k_norm_g, rel_bias, attn_out_norm_g, conv_w, a_log, dt_bias, dn_out_norm_g, w_out, norm_ffn_g, w_gate, w_up, w_down):
    b, t, d = x.shape
    outs = []
    for bi in range(b):
        x2 = x[bi]
        for l in range(w_in.shape[0]):
            x2 = _layer(x2, norm_mix_g[l], w_in[l], attn_q_norm_g[l], attn_k_norm_g[l], rel_bias[l],
                        attn_out_norm_g[l], conv_w[l], a_log[l], dt_bias[l], dn_out_norm_g[l],
                        w_out[l], norm_ffn_g[l], w_gate[l], w_up[l], w_down[l])
        outs.append(x2)
    return jnp.stack(outs, axis=0) if b > 1 else outs[0][None]
```

```python
import functools

import jax
import jax.numpy as jnp
from jax import lax
from jax.experimental import pallas as pl
from jax.experimental.pallas import tpu as pltpu

F32 = jnp.float32
BF16 = jnp.bfloat16

D_MODEL = 1024
CHUNK = 64
ATTN_HEAD_DIM = 64
ATTN_HEADS = 8
ATTN_WIDTH = ATTN_HEADS * ATTN_HEAD_DIM
LEFT_CHUNKS = 8
REL_CLIP = 128
DN_HEAD_DIM = 128
DN_HEADS = 4
DN_WIDTH = DN_HEADS * DN_HEAD_DIM
CONV_K = 4
EPS = 1e-6
NEG_INF = -1e30
LOG2E = 1.4426950408889634

LANES = 128
SUBLANES = 8
MXU_N = 256
VMEM_LIMIT = 56 * 1024 * 1024

PROJ_TM = 512
CONV_ROWS = 64
ATTN_TB = 512
PAIR = 2 * CHUNK
WIN = PAIR + LEFT_CHUNKS * CHUNK
DN_TB = 512
DN_NC = DN_TB // CHUNK
FFN_TM = 256


def _resident(shape):
    return pl.BlockSpec(shape, lambda i: (0,) * len(shape), pipeline_mode=pl.Buffered(1))


def _dot(a, b):
    return jnp.dot(a, b, preferred_element_type=F32)


def _dot_nt(a, b):
    return lax.dot_general(a, b, (((1,), (1,)), ((), ())), preferred_element_type=F32)


def _bdot(a, b):
    return lax.dot_general(a, b, (((2,), (1,)), ((0,), (0,))), preferred_element_type=F32)


def _bdot_nt(a, b):
    return lax.dot_general(a, b, (((2,), (2,)), ((0,), (0,))), preferred_element_type=F32)


def _silu(x):
    return x * jax.nn.sigmoid(x)


def _in_proj_kernel(x_ref, g_ref, wa_ref, wd_ref, wz_ref, wg_ref, cw_ref, alog_ref, dtb_ref,
                    a_ref, dq_ref, dk_ref, dv_ref, z_ref, gates_ref, buf_ref):
    i = pl.program_id(0)
    hd = DN_HEAD_DIM
    pad = SUBLANES
    x = x_ref[...]
    ms = jnp.mean(x * x, axis=-1, keepdims=True)
    h = ((x * lax.rsqrt(ms + EPS)) * g_ref[...]).astype(BF16)

    @pl.when(i == 0)
    def _zero_tail():
        buf_ref[0:pad, :] = jnp.zeros((pad, 3 * DN_WIDTH), F32)

    @pl.when(i > 0)
    def _carry_tail():
        buf_ref[0:pad, :] = buf_ref[PROJ_TM:PROJ_TM + pad, :]

    def conv_tile(c):
        cols = slice(c * hd, (c + 1) * hd)
        part, hh = divmod(c, DN_HEADS)
        hcols = slice(hh * hd, (hh + 1) * hd)
        for r0 in range(0, PROJ_TM, CONV_ROWS):
            rows = slice(r0, r0 + CONV_ROWS)
            y = jnp.zeros((CONV_ROWS, hd), F32)
            for tap in range(CONV_K):
                lo = pad - (CONV_K - 1) + tap + r0
                y = y + cw_ref[tap:tap + 1, cols] * buf_ref[lo:lo + CONV_ROWS, cols]
            y = _silu(y)
            if part == 2:
                dv_ref[rows, hcols] = y
            else:
                yn = y * lax.rsqrt(jnp.sum(y * y, axis=-1, keepdims=True) + EPS)
                if part == 0:
                    dq_ref[rows, hcols] = yn * (hd ** -0.5)
                else:
                    dk_ref[rows, hcols] = yn

    ba = _dot(h, wg_ref[...])
    beta = jax.nn.sigmoid(ba)
    y = ba + dtb_ref[...]
    g = -jnp.exp(alog_ref[...]) * (jnp.maximum(y, 0.0) + jnp.log1p(jnp.exp(-jnp.abs(y))))
    pos = lax.broadcasted_iota(jnp.int32, g.shape, 0) % CHUNK
    shift = 1
    while shift < CHUNK:
        g = g + jnp.where(pos >= shift, pltpu.roll(g, shift, 0), 0.0)
        shift *= 2
    lane = lax.broadcasted_iota(jnp.int32, g.shape, 1)
    gates_ref[...] = jnp.where(lane < DN_HEADS, beta, g)

    for j in range(3 * DN_WIDTH // MXU_N):
        cols = slice(j * MXU_N, (j + 1) * MXU_N)
        buf_ref[pad:pad + PROJ_TM, cols] = _dot(h, wd_ref[:, cols])
        a_ref[:, cols] = _dot(h, wa_ref[:, cols])
        for c in range(j * MXU_N // hd, (j + 1) * MXU_N // hd):
            conv_tile(c)
    z_ref[...] = _silu(_dot(h, wz_ref[...]))


def _in_proj(x2, g, wa, wd, wz, wg, conv_w, alog_row, dtb_row):
    t = x2.shape[0]
    row = lambda i: (i, 0)
    widths = (wa.shape[1], DN_WIDTH, DN_WIDTH, DN_WIDTH, wz.shape[1], wg.shape[1])
    return pl.pallas_call(
        _in_proj_kernel,
        grid=(t // PROJ_TM,),
        in_specs=[
            pl.BlockSpec((PROJ_TM, D_MODEL), row),
            _resident((1, D_MODEL)),
            _resident(wa.shape), _resident(wd.shape), _resident(wz.shape), _resident(wg.shape),
            _resident(conv_w.shape), _resident((1, LANES)), _resident((1, LANES)),
        ],
        out_specs=[pl.BlockSpec((PROJ_TM, w), row) for w in widths],
        out_shape=[jax.ShapeDtypeStruct((t, w), F32) for w in widths],
        scratch_shapes=[pltpu.VMEM((PROJ_TM + 2 * SUBLANES, 3 * DN_WIDTH), F32)],
        compiler_params=pltpu.CompilerParams(
            dimension_semantics=("arbitrary",), vmem_limit_bytes=VMEM_LIMIT),
        name="in_proj",
    )(x2, g, wa, wd, wz, wg, conv_w, alog_row, dtb_row)


def _head_mean_sq(x, bd):
    return _dot((x * x).astype(BF16), bd) * (1.0 / ATTN_HEAD_DIM)


def _attn_kernel(q_ref, k_ref, v_ref, qg_ref, kg_ref, e_ref, bd_ref, og_ref, o_ref,
                 bias_ref, qn_ref, kn_ref, vn_ref, ob_ref):
    i = pl.program_id(0)
    n_hp = ATTN_HEADS // 2

    @pl.when(i == 0)
    def _init():
        r = lax.broadcasted_iota(jnp.int32, (PAIR, WIN), 0)
        j = lax.broadcasted_iota(jnp.int32, (PAIR, WIN), 1)
        first_key = jnp.where(r < CHUNK, 0, CHUNK)
        valid = (j >= first_key) & (j < first_key + WIN - CHUNK)
        for h in range(ATTN_HEADS):
            eb = jnp.broadcast_to(e_ref[h:h + 1, :], (PAIR, WIN + PAIR))
            rolled = pltpu.roll(eb, 0, 1, stride=1, stride_axis=0)
            tab = jnp.where(valid, rolled[:, PAIR:] * LOG2E, NEG_INF)
            bias_ref[h // 2, (h % 2) * PAIR:(h % 2 + 1) * PAIR, :] = tab
        kn_ref[0:ATTN_TB, :] = jnp.zeros((ATTN_TB, ATTN_WIDTH), BF16)
        vn_ref[0:ATTN_TB, :] = jnp.zeros((ATTN_TB, ATTN_WIDTH), BF16)

    @pl.when(i > 0)
    def _shift():
        kn_ref[0:ATTN_TB, :] = kn_ref[ATTN_TB:2 * ATTN_TB, :]
        vn_ref[0:ATTN_TB, :] = vn_ref[ATTN_TB:2 * ATTN_TB, :]

    bd = bd_ref[...]
    q = q_ref[...]
    qn = (q * lax.rsqrt(_head_mean_sq(q, bd) + EPS)) * qg_ref[...]
    qn_ref[...] = (qn * (ATTN_HEAD_DIM ** -0.5 * LOG2E)).astype(BF16)
    k = k_ref[...]
    kn = (k * lax.rsqrt(_head_mean_sq(k, bd) + EPS)) * kg_ref[...]
    kn_ref[ATTN_TB:2 * ATTN_TB, :] = kn.astype(BF16)
    vn_ref[ATTN_TB:2 * ATTN_TB, :] = v_ref[...].astype(BF16)

    lane = lax.broadcasted_iota(jnp.int32, (PAIR, LANES), 1)
    first_half = lane < ATTN_HEAD_DIM

    def scores(q0, hp, first_block):
        cols = slice(hp * LANES, (hp + 1) * LANES)
        q2 = qn_ref[pl.ds(q0, PAIR), cols]
        zero = jnp.zeros_like(q2)
        lhs = jnp.concatenate(
            [jnp.where(first_half, q2, zero), jnp.where(first_half, zero, q2)], axis=0)
        s = _dot_nt(lhs, kn_ref[pl.ds(q0, WIN), cols]) + bias_ref[hp]
        if first_block:
            jj = lax.broadcasted_iota(jnp.int32, (2 * PAIR, WIN), 1)
            s = jnp.where(jj + q0 >= LEFT_CHUNKS * CHUNK, s, NEG_INF)
        return s

    def finish(s, q0, hp):
        cols = slice(hp * LANES, (hp + 1) * LANES)
        m = jnp.max(s, axis=-1, keepdims=True)
        e = jnp.exp2(s - m)
        l = jnp.sum(e, axis=-1, keepdims=True)
        pv = _dot(e.astype(BF16), vn_ref[pl.ds(q0, WIN), cols]) / l
        ob_ref[pl.ds(q0, PAIR), cols] = jnp.where(first_half, pv[:PAIR], pv[PAIR:])

    @pl.when(i == 0)
    def _first():
        def pair_body(p, carry):
            q0 = pl.multiple_of(p * PAIR, PAIR)
            for hp in range(n_hp):
                finish(scores(q0, hp, True), q0, hp)
            return carry
        lax.fori_loop(0, ATTN_TB // PAIR, pair_body, 0)

    @pl.when(i > 0)
    def _rest():
        items = [(p * PAIR, hp) for p in range(ATTN_TB // PAIR) for hp in range(n_hp)]
        s_next = scores(*items[0], False)
        for j, item in enumerate(items):
            s_cur = s_next
            if j + 1 < len(items):
                s_next = scores(*items[j + 1], False)
            finish(s_cur, *item)

    a = ob_ref[...]
    ms = jnp.mean(a * a, axis=-1, keepdims=True)
    o_ref[...] = (a * lax.rsqrt(ms + EPS)) * og_ref[...]


def _attention(a_qkv, qg, kg, e_tab, bd, og):
    t = a_qkv.shape[0]
    n_hp = ATTN_HEADS // 2
    return pl.pallas_call(
        _attn_kernel,
        grid=(t // ATTN_TB,),
        in_specs=[
            pl.BlockSpec((ATTN_TB, ATTN_WIDTH), lambda i: (i, 0)),
            pl.BlockSpec((ATTN_TB, ATTN_WIDTH), lambda i: (i, 1)),
            pl.BlockSpec((ATTN_TB, ATTN_WIDTH), lambda i: (i, 2)),
            _resident((1, ATTN_WIDTH)), _resident((1, ATTN_WIDTH)),
            _resident(e_tab.shape), _resident(bd.shape), _resident((1, ATTN_WIDTH)),
        ],
        out_specs=pl.BlockSpec((ATTN_TB, ATTN_WIDTH), lambda i: (i, 0)),
        out_shape=jax.ShapeDtypeStruct((t, ATTN_WIDTH), F32),
        scratch_shapes=[
            pltpu.VMEM((n_hp, 2 * PAIR, WIN), F32),
            pltpu.VMEM((ATTN_TB, ATTN_WIDTH), BF16),
            pltpu.VMEM((2 * ATTN_TB, ATTN_WIDTH), BF16),
            pltpu.VMEM((2 * ATTN_TB, ATTN_WIDTH), BF16),
            pltpu.VMEM((ATTN_TB, ATTN_WIDTH), F32),
        ],
        compiler_params=pltpu.CompilerParams(
            dimension_semantics=("arbitrary",), vmem_limit_bytes=VMEM_LIMIT),
        name="band_attn",
    )(a_qkv, a_qkv, a_qkv, qg, kg, e_tab, bd, og)


def _deltanet_kernel(qn_ref, kn_ref, vn_ref, z_ref, gcol_ref, og_ref, o_ref,
                     qs_ref, o0_ref, m_ref, bb_ref, egl_ref, s_ref, ob_ref):
    i = pl.program_id(0)
    hd = DN_HEAD_DIM

    @pl.when(i == 0)
    def _init():
        s_ref[...] = jnp.zeros(s_ref.shape, F32)

    def per_head(x, width, off=0):
        x3 = x.reshape(DN_NC, CHUNK, x.shape[-1])
        return jnp.concatenate(
            [x3[:, :, off + h * width:off + (h + 1) * width] for h in range(DN_HEADS)], axis=0)

    k = per_head(kn_ref[...], hd)
    q = per_head(qn_ref[...], hd)
    v = per_head(vn_ref[...], hd)
    gcol = gcol_ref[...]
    beta = per_head(gcol, 1)
    gc = per_head(gcol, 1, DN_HEADS)
    grow = jnp.swapaxes(gcol.reshape(DN_NC, CHUNK, LANES), 1, 2)
    gcr = jnp.concatenate(
        [grow[:, DN_HEADS + h:DN_HEADS + h + 1, :] for h in range(DN_HEADS)], axis=0)
    g_last = gcr[:, :, CHUNK - 1:CHUNK]

    ri = lax.broadcasted_iota(jnp.int32, (1, CHUNK, CHUNK), 1)
    ci = lax.broadcasted_iota(jnp.int32, (1, CHUNK, CHUNK), 2)
    tri_incl = ri >= ci
    decay = jnp.where(tri_incl, jnp.exp(jnp.where(tri_incl, gc - gcr, 0.0)), 0.0)
    kb = k * beta
    kbf = k.astype(BF16)
    kkqk = _bdot_nt(jnp.concatenate([kb, q], axis=1).astype(BF16), kbf)
    lmat = jnp.where(ri > ci, kkqk[:, :CHUNK] * decay, 0.0)
    intra = kkqk[:, CHUNK:] * decay
    x = jnp.where(ri == ci, 1.0, 0.0) - lmat
    lb = lmat.astype(BF16)
    pw = _bdot(lb, lb)
    n_levels = 5
    for lvl in range(n_levels):
        pwb = pw.astype(BF16)
        if lvl + 1 < n_levels:
            r = _bdot(jnp.concatenate([pwb, x.astype(BF16)], axis=1), pwb)
            pw = r[:, :CHUNK]
            x = x + r[:, CHUNK:]
        else:
            x = x + _bdot(x.astype(BF16), pwb)
    eg = jnp.exp(gc)
    rhs = jnp.concatenate([v * beta, kb * eg], axis=-1).astype(BF16)
    sol = _bdot(x.astype(BF16), rhs)
    solb = sol.astype(BF16)
    iuw = _bdot(intra.astype(BF16), solb)
    kd = k * jnp.exp(g_last - gc)
    kuw = _bdot(jnp.swapaxes(kd, 1, 2).astype(BF16), solb)
    qs_ref[...] = (q * eg - iuw[:, :, hd:]).astype(BF16)
    o0_ref[...] = iuw[:, :, :hd]
    m_ref[...] = kuw[:, :, hd:].astype(BF16)
    bb_ref[...] = kuw[:, :, :hd]
    egl_ref[...] = jnp.broadcast_to(jnp.exp(g_last), (DN_NC * DN_HEADS, 1, hd))

    def scan_body(n, carry):
        r0 = pl.multiple_of(n * CHUNK, CHUNK)
        for h in range(DN_HEADS):
            b = h * DN_NC + n
            s = s_ref[h]
            sb = s.astype(BF16)
            ob_ref[pl.ds(r0, CHUNK), h * hd:(h + 1) * hd] = _dot(qs_ref[b], sb) + o0_ref[b]
            s_ref[h] = (s * egl_ref[b] - _dot(m_ref[b], sb)) + bb_ref[b]
        return carry
    lax.fori_loop(0, DN_NC, scan_body, 0)

    for h in range(DN_HEADS):
        hcols = slice(h * hd, (h + 1) * hd)
        o = ob_ref[:, hcols]
        ms = jnp.mean(o * o, axis=-1, keepdims=True)
        o_ref[:, hcols] = ((o * lax.rsqrt(ms + EPS)) * og_ref[:, hcols]) * z_ref[:, hcols]


def _deltanet(dq, dk, dv, zs, gates, og):
    t = dq.shape[0]
    hd = DN_HEAD_DIM
    nb = DN_NC * DN_HEADS
    row = lambda i: (i, 0)
    return pl.pallas_call(
        _deltanet_kernel,
        grid=(t // DN_TB,),
        in_specs=[
            pl.BlockSpec((DN_TB, DN_WIDTH), row), pl.BlockSpec((DN_TB, DN_WIDTH), row),
            pl.BlockSpec((DN_TB, DN_WIDTH), row), pl.BlockSpec((DN_TB, DN_WIDTH), row),
            pl.BlockSpec((DN_TB, LANES), row),
            _resident((1, DN_WIDTH)),
        ],
        out_specs=pl.BlockSpec((DN_TB, DN_WIDTH), row),
        out_shape=jax.ShapeDtypeStruct((t, DN_WIDTH), F32),
        scratch_shapes=[
            pltpu.VMEM((nb, CHUNK, hd), BF16),
            pltpu.VMEM((nb, CHUNK, hd), F32),
            pltpu.VMEM((nb, hd, hd), BF16),
            pltpu.VMEM((nb, hd, hd), F32),
            pltpu.VMEM((nb, 1, hd), F32),
            pltpu.VMEM((DN_HEADS, hd, hd), F32),
            pltpu.VMEM((DN_TB, DN_WIDTH), F32),
        ],
        compiler_params=pltpu.CompilerParams(
            dimension_semantics=("arbitrary",), vmem_limit_bytes=VMEM_LIMIT),
        name="deltanet",
    )(dq, dk, dv, zs, gates, og)


def _out_ffn_kernel(x_ref, a_ref, d_ref, wo_ref, g_ref, wg_ref, wu_ref, wd_ref, o_ref):
    wo_a = wo_ref[0:ATTN_WIDTH, :]
    wo_d = wo_ref[ATTN_WIDTH:D_MODEL, :]
    x1 = x_ref[...] + _dot(a_ref[...].astype(BF16), wo_a) + _dot(d_ref[...].astype(BF16), wo_d)
    ms = jnp.mean(x1 * x1, axis=-1, keepdims=True)
    h = ((x1 * lax.rsqrt(ms + EPS)) * g_ref[...]).astype(BF16)
    gate = _dot(h, wg_ref[...])
    up = _dot(h, wu_ref[...])
    y = (_silu(gate) * up).astype(BF16)
    o_ref[...] = x1 + _dot(y, wd_ref[...])


def _out_ffn(x2, a_out, d_out, wo, g, wg, wu, wd):
    t = x2.shape[0]
    row = lambda i: (i, 0)
    return pl.pallas_call(
        _out_ffn_kernel,
        grid=(t // FFN_TM,),
        in_specs=[
            pl.BlockSpec((FFN_TM, D_MODEL), row),
            pl.BlockSpec((FFN_TM, ATTN_WIDTH), row),
            pl.BlockSpec((FFN_TM, DN_WIDTH), row),
            _resident(wo.shape), _resident((1, D_MODEL)),
            _resident(wg.shape), _resident(wu.shape), _resident(wd.shape),
        ],
        out_specs=pl.BlockSpec((FFN_TM, D_MODEL), row),
        out_shape=jax.ShapeDtypeStruct((t, D_MODEL), F32),
        compiler_params=pltpu.CompilerParams(
            dimension_semantics=("arbitrary",), vmem_limit_bytes=VMEM_LIMIT),
        name="out_ffn",
    )(x2, a_out, d_out, wo, g, wg, wu, wd)


def _layer(x2, norm_mix_g, w_in, attn_q_norm_g, attn_k_norm_g, rel_bias, attn_out_norm_g,
           conv_w, a_log, dt_bias, dn_out_norm_g, w_out, norm_ffn_g, w_gate, w_up, w_down):
    na = 3 * ATTN_WIDTH
    nd = 3 * DN_WIDTH
    wb = w_in.astype(BF16)
    wa = wb[:, :na]
    wd = wb[:, na:na + nd]
    wz = wb[:, na + nd:na + nd + DN_WIDTH]
    wg = jnp.pad(wb[:, na + nd + DN_WIDTH:], ((0, 0), (0, LANES - 2 * DN_HEADS)))
    gate_lanes = (DN_HEADS, LANES - 2 * DN_HEADS)
    alog_row = jnp.pad(a_log, gate_lanes)[None, :]
    dtb_row = jnp.pad(dt_bias, gate_lanes)[None, :]
    a_qkv, dq, dk, dv, zs, gates = _in_proj(
        x2, norm_mix_g[None, :], wa, wd, wz, wg, conv_w, alog_row, dtb_row)

    qg = jnp.tile(attn_q_norm_g, ATTN_HEADS)[None, :]
    kg = jnp.tile(attn_k_norm_g, ATTN_HEADS)[None, :]
    rb_t = rel_bias.T
    far = jnp.broadcast_to(rb_t[:, 2 * REL_CLIP:], (ATTN_HEADS, WIN + PAIR - 2 * REL_CLIP))
    e_tab = jnp.concatenate([far, rb_t[:, 2 * REL_CLIP:0:-1]], axis=1)
    hid = jnp.arange(ATTN_WIDTH) // ATTN_HEAD_DIM
    bd = (hid[:, None] == hid[None, :]).astype(BF16)
    a_out = _attention(a_qkv, qg, kg, e_tab, bd, attn_out_norm_g[None, :])

    og = jnp.tile(dn_out_norm_g, DN_HEADS)[None, :]
    d_out = _deltanet(dq, dk, dv, zs, gates, og)

    return _out_ffn(x2, a_out, d_out, w_out.astype(BF16), norm_ffn_g[None, :],
                    w_gate.astype(BF16), w_up.astype(BF16), w_down.astype(BF16))


def kernel(x, norm_mix_g, w_in, attn_q_norm_g, attn_k_norm_g, rel_bias, attn_out_norm_g, conv_w, a_log, dt_bias, dn_out_norm_g, w_out, norm_ffn_g, w_gate, w_up, w_down):
    b, t, d = x.shape
    outs = []
    for bi in range(b):
        x2 = x[bi]
        for l in range(w_in.shape[0]):
            x2 = _layer(x2, norm_mix_g[l], w_in[l], attn_q_norm_g[l], attn_k_norm_g[l], rel_bias[l],
                        attn_out_norm_g[l], conv_w[l], a_log[l], dt_bias[l], dn_out_norm_g[l],
                        w_out[l], norm_ffn_g[l], w_gate[l], w_up[l], w_down[l])
        outs.append(x2)
    return jnp.stack(outs, axis=0) if b > 1 else outs[0][None]
```

```python
import functools

import jax
import jax.numpy as jnp
from jax import lax
from jax.experimental import pallas as pl
from jax.experimental.pallas import tpu as pltpu

F32 = jnp.float32
BF16 = jnp.bfloat16

D_MODEL = 1024
CHUNK = 64
ATTN_HEAD_DIM = 64
ATTN_HEADS = 8
ATTN_WIDTH = ATTN_HEADS * ATTN_HEAD_DIM
LEFT_CHUNKS = 8
REL_CLIP = 128
DN_HEAD_DIM = 128
DN_HEADS = 4
DN_WIDTH = DN_HEADS * DN_HEAD_DIM
CONV_K = 4
EPS = 1e-6
NEG_INF = -1e30
LOG2E = 1.4426950408889634

LANES = 128
SUBLANES = 8
MXU_N = 256
VMEM_LIMIT = 56 * 1024 * 1024

PROJ_TM = 512
CONV_ROWS = 64
ATTN_TB = 512
PAIR = 2 * CHUNK
WIN = PAIR + LEFT_CHUNKS * CHUNK
DN_TB = 1024
DN_GROUPS = (4, 4, 4, 4)
DN_SCAN_EVERY = 1
FFN_TM = 512
FFN_ROWS = 256


def _resident(shape):
    return pl.BlockSpec(shape, lambda i: (0,) * len(shape), pipeline_mode=pl.Buffered(1))


def _dot(a, b):
    return jnp.dot(a, b, preferred_element_type=F32)


def _dot_nt(a, b):
    return lax.dot_general(a, b, (((1,), (1,)), ((), ())), preferred_element_type=F32)


def _bdot(a, b):
    return lax.dot_general(a, b, (((2,), (1,)), ((0,), (0,))), preferred_element_type=F32)


def _bdot_nt(a, b):
    return lax.dot_general(a, b, (((2,), (2,)), ((0,), (0,))), preferred_element_type=F32)


def _silu(x):
    return x * jax.nn.sigmoid(x)


def _in_proj_kernel(x_ref, xn_ref, g_ref, w_ref, cw_ref, alog_ref, dtb_ref,
                    a_ref, dq_ref, dk_ref, dv_ref, z_ref, gates_ref, buf_ref, h0_ref, h1_ref):
    i = pl.program_id(0)
    hd = DN_HEAD_DIM
    pad = SUBLANES
    col_d = 3 * ATTN_WIDTH
    col_z = col_d + 3 * DN_WIDTH
    col_g = col_z + DN_WIDTH

    def normed(x):
        ms = jnp.mean(x * x, axis=-1, keepdims=True)
        return ((x * lax.rsqrt(ms + EPS)) * g_ref[...]).astype(BF16)

    @pl.when(i == 0)
    def _first():
        buf_ref[0:pad, :] = jnp.zeros((pad, 3 * DN_WIDTH), F32)
        h0_ref[...] = normed(x_ref[...])

    @pl.when(i > 0)
    def _carry_tail():
        buf_ref[0:pad, :] = buf_ref[PROJ_TM:PROJ_TM + pad, :]

    def conv_tile(c):
        cols = slice(c * hd, (c + 1) * hd)
        part, hh = divmod(c, DN_HEADS)
        hcols = slice(hh * hd, (hh + 1) * hd)
        for r0 in range(0, PROJ_TM, CONV_ROWS):
            rows = slice(r0, r0 + CONV_ROWS)
            y = jnp.zeros((CONV_ROWS, hd), F32)
            for tap in range(CONV_K):
                lo = pad - (CONV_K - 1) + tap + r0
                y = y + cw_ref[tap:tap + 1, cols] * buf_ref[lo:lo + CONV_ROWS, cols]
            y = _silu(y)
            if part == 2:
                dv_ref[rows, hcols] = y
            else:
                yn = y * lax.rsqrt(jnp.sum(y * y, axis=-1, keepdims=True) + EPS)
                if part == 0:
                    dq_ref[rows, hcols] = yn * (hd ** -0.5)
                else:
                    dk_ref[rows, hcols] = yn

    def project(h_ref, h_next_ref):
        h_next_ref[...] = normed(xn_ref[...])
        h = h_ref[...]

        ba = _dot(h, w_ref[:, col_g:col_g + 2 * DN_HEADS])
        beta = jax.nn.sigmoid(ba)
        y = ba + dtb_ref[...]
        g = -jnp.exp(alog_ref[...]) * (jnp.maximum(y, 0.0) + jnp.log1p(jnp.exp(-jnp.abs(y))))
        pos = lax.broadcasted_iota(jnp.int32, g.shape, 0) % CHUNK
        shift = 1
        while shift < CHUNK:
            g = g + jnp.where(pos >= shift, pltpu.roll(g, shift, 0), 0.0)
            shift *= 2
        lane = lax.broadcasted_iota(jnp.int32, g.shape, 1)
        gates_ref[...] = jnp.where(lane < DN_HEADS, beta, g)

        for j in range(3 * DN_WIDTH // MXU_N):
            cols = slice(j * MXU_N, (j + 1) * MXU_N)
            buf_ref[pad:pad + PROJ_TM, cols] = _dot(h, w_ref[:, col_d + j * MXU_N:col_d + (j + 1) * MXU_N])
            a_ref[:, cols] = _dot(h, w_ref[:, cols])
            for c in range(j * MXU_N // hd, (j + 1) * MXU_N // hd):
                conv_tile(c)
        z_ref[...] = _silu(_dot(h, w_ref[:, col_z:col_g]))

    @pl.when(i % 2 == 0)
    def _even():
        project(h0_ref, h1_ref)

    @pl.when(i % 2 == 1)
    def _odd():
        project(h1_ref, h0_ref)


def _in_proj(x2, g, wb, conv_w, alog_row, dtb_row):
    t = x2.shape[0]
    n = t // PROJ_TM
    row = lambda i: (i, 0)
    widths = (3 * ATTN_WIDTH, DN_WIDTH, DN_WIDTH, DN_WIDTH, DN_WIDTH, 2 * DN_HEADS)
    return pl.pallas_call(
        _in_proj_kernel,
        grid=(n,),
        in_specs=[
            pl.BlockSpec((PROJ_TM, D_MODEL), row),
            pl.BlockSpec((PROJ_TM, D_MODEL), lambda i: (jnp.minimum(i + 1, n - 1), 0)),
            _resident((1, D_MODEL)), _resident(wb.shape),
            _resident(conv_w.shape), _resident((1, 2 * DN_HEADS)), _resident((1, 2 * DN_HEADS)),
        ],
        out_specs=[pl.BlockSpec((PROJ_TM, w), row) for w in widths],
        out_shape=[jax.ShapeDtypeStruct((t, w), F32) for w in widths],
        scratch_shapes=[
            pltpu.VMEM((PROJ_TM + 2 * SUBLANES, 3 * DN_WIDTH), F32),
            pltpu.VMEM((PROJ_TM, D_MODEL), BF16), pltpu.VMEM((PROJ_TM, D_MODEL), BF16),
        ],
        compiler_params=pltpu.CompilerParams(
            dimension_semantics=("arbitrary",), vmem_limit_bytes=VMEM_LIMIT),
        name="in_proj",
    )(x2, x2, g, wb, conv_w, alog_row, dtb_row)


def _head_mean_sq(x, bd):
    return _dot((x * x).astype(BF16), bd) * (1.0 / ATTN_HEAD_DIM)


def _attn_kernel(q_ref, k_ref, v_ref, qg_ref, kg_ref, e_ref, bd_ref, og_ref, o_ref,
                 bias_ref, qn_ref, kn_ref, vn_ref, ob_ref):
    i = pl.program_id(0)
    n_hp = ATTN_HEADS // 2

    @pl.when(i == 0)
    def _init():
        r = lax.broadcasted_iota(jnp.int32, (PAIR, WIN), 0)
        j = lax.broadcasted_iota(jnp.int32, (PAIR, WIN), 1)
        first_key = jnp.where(r < CHUNK, 0, CHUNK)
        valid = (j >= first_key) & (j < first_key + WIN - CHUNK)
        for h in range(ATTN_HEADS):
            eb = jnp.broadcast_to(e_ref[h:h + 1, :], (PAIR, WIN + PAIR))
            rolled = pltpu.roll(eb, 0, 1, stride=1, stride_axis=0)
            tab = jnp.where(valid, rolled[:, PAIR:] * LOG2E, NEG_INF)
            bias_ref[h // 2, (h % 2) * PAIR:(h % 2 + 1) * PAIR, :] = tab
        kn_ref[0:ATTN_TB, :] = jnp.zeros((ATTN_TB, ATTN_WIDTH), BF16)
        vn_ref[0:ATTN_TB, :] = jnp.zeros((ATTN_TB, ATTN_WIDTH), BF16)

    @pl.when(i > 0)
    def _shift():
        kn_ref[0:ATTN_TB, :] = kn_ref[ATTN_TB:2 * ATTN_TB, :]
        vn_ref[0:ATTN_TB, :] = vn_ref[ATTN_TB:2 * ATTN_TB, :]

    bd = bd_ref[...]
    q = q_ref[...]
    qn = (q * lax.rsqrt(_head_mean_sq(q, bd) + EPS)) * qg_ref[...]
    qn_ref[...] = (qn * (ATTN_HEAD_DIM ** -0.5 * LOG2E)).astype(BF16)
    k = k_ref[...]
    kn = (k * lax.rsqrt(_head_mean_sq(k, bd) + EPS)) * kg_ref[...]
    kn_ref[ATTN_TB:2 * ATTN_TB, :] = kn.astype(BF16)
    vn_ref[ATTN_TB:2 * ATTN_TB, :] = v_ref[...].astype(BF16)

    lane = lax.broadcasted_iota(jnp.int32, (PAIR, LANES), 1)
    first_half = lane < ATTN_HEAD_DIM

    def scores(q0, hp, first_block):
        cols = slice(hp * LANES, (hp + 1) * LANES)
        q2 = qn_ref[pl.ds(q0, PAIR), cols]
        zero = jnp.zeros_like(q2)
        lhs = jnp.concatenate(
            [jnp.where(first_half, q2, zero), jnp.where(first_half, zero, q2)], axis=0)
        s = _dot_nt(lhs, kn_ref[pl.ds(q0, WIN), cols]) + bias_ref[hp]
        if first_block:
            jj = lax.broadcasted_iota(jnp.int32, (2 * PAIR, WIN), 1)
            s = jnp.where(jj + q0 >= LEFT_CHUNKS * CHUNK, s, NEG_INF)
        return s

    def finish(s, q0, hp):
        cols = slice(hp * LANES, (hp + 1) * LANES)
        m = jnp.max(s, axis=-1, keepdims=True)
        e = jnp.exp2(s - m)
        l = jnp.sum(e, axis=-1, keepdims=True)
        pv = _dot(e.astype(BF16), vn_ref[pl.ds(q0, WIN), cols]) / l
        ob_ref[pl.ds(q0, PAIR), cols] = jnp.where(first_half, pv[:PAIR], pv[PAIR:])

    @pl.when(i == 0)
    def _first():
        def pair_body(p, carry):
            q0 = pl.multiple_of(p * PAIR, PAIR)
            for hp in range(n_hp):
                finish(scores(q0, hp, True), q0, hp)
            return carry
        lax.fori_loop(0, ATTN_TB // PAIR, pair_body, 0)

    @pl.when(i > 0)
    def _rest():
        items = [(p * PAIR, hp) for p in range(ATTN_TB // PAIR) for hp in range(n_hp)]
        s_next = scores(*items[0], False)
        for j, item in enumerate(items):
            s_cur = s_next
            if j + 1 < len(items):
                s_next = scores(*items[j + 1], False)
            finish(s_cur, *item)

    a = ob_ref[...]
    ms = jnp.mean(a * a, axis=-1, keepdims=True)
    o_ref[...] = (a * lax.rsqrt(ms + EPS)) * og_ref[...]


def _attention(a_qkv, qg, kg, e_tab, bd, og):
    t = a_qkv.shape[0]
    n_hp = ATTN_HEADS // 2
    return pl.pallas_call(
        _attn_kernel,
        grid=(t // ATTN_TB,),
        in_specs=[
            pl.BlockSpec((ATTN_TB, ATTN_WIDTH), lambda i: (i, 0)),
            pl.BlockSpec((ATTN_TB, ATTN_WIDTH), lambda i: (i, 1)),
            pl.BlockSpec((ATTN_TB, ATTN_WIDTH), lambda i: (i, 2)),
            _resident((1, ATTN_WIDTH)), _resident((1, ATTN_WIDTH)),
            _resident(e_tab.shape), _resident(bd.shape), _resident((1, ATTN_WIDTH)),
        ],
        out_specs=pl.BlockSpec((ATTN_TB, ATTN_WIDTH), lambda i: (i, 0)),
        out_shape=jax.ShapeDtypeStruct((t, ATTN_WIDTH), F32),
        scratch_shapes=[
            pltpu.VMEM((n_hp, 2 * PAIR, WIN), F32),
            pltpu.VMEM((ATTN_TB, ATTN_WIDTH), BF16),
            pltpu.VMEM((2 * ATTN_TB, ATTN_WIDTH), BF16),
            pltpu.VMEM((2 * ATTN_TB, ATTN_WIDTH), BF16),
            pltpu.VMEM((ATTN_TB, ATTN_WIDTH), F32),
        ],
        compiler_params=pltpu.CompilerParams(
            dimension_semantics=("arbitrary",), vmem_limit_bytes=VMEM_LIMIT),
        name="band_attn",
    )(a_qkv, a_qkv, a_qkv, qg, kg, e_tab, bd, og)


def _deltanet_kernel(qn_ref, kn_ref, vn_ref, z_ref, gcol_ref, og_ref, o_ref, s_ref, ob_ref):
    i = pl.program_id(0)
    hd = DN_HEAD_DIM

    @pl.when(i == 0)
    def _init():
        s_ref[...] = jnp.zeros(s_ref.shape, F32)

    ri = lax.broadcasted_iota(jnp.int32, (1, CHUNK, CHUNK), 1)
    ci = lax.broadcasted_iota(jnp.int32, (1, CHUNK, CHUNK), 2)
    tri_incl = ri >= ci

    def local_stages(c0, nc, out):
        rows = slice(c0 * CHUNK, (c0 + nc) * CHUNK)

        def per_head(x, width, off=0):
            x3 = x.reshape(nc, CHUNK, x.shape[-1])
            return jnp.concatenate(
                [x3[:, :, off + h * width:off + (h + 1) * width] for h in range(DN_HEADS)], axis=0)

        k = per_head(kn_ref[rows, :], hd)
        q = per_head(qn_ref[rows, :], hd)
        v = per_head(vn_ref[rows, :], hd)
        gcol = gcol_ref[rows, :]
        beta = per_head(gcol, 1)
        gc = per_head(gcol, 1, DN_HEADS)
        grow = jnp.swapaxes(gcol.reshape(nc, CHUNK, 2 * DN_HEADS), 1, 2)
        gcr = jnp.concatenate(
            [grow[:, DN_HEADS + h:DN_HEADS + h + 1, :] for h in range(DN_HEADS)], axis=0)
        g_last = gcr[:, :, CHUNK - 1:CHUNK]
        decay = jnp.where(tri_incl, jnp.exp(jnp.where(tri_incl, gc - gcr, 0.0)), 0.0)
        kb = k * beta
        kbf = k.astype(BF16)
        kkqk = _bdot_nt(jnp.concatenate([kb, q], axis=1).astype(BF16), kbf)
        yield
        lmat = jnp.where(ri > ci, kkqk[:, :CHUNK] * decay, 0.0)
        intra = kkqk[:, CHUNK:] * decay
        x = jnp.where(ri == ci, 1.0, 0.0) - lmat
        lb = lmat.astype(BF16)
        pw = _bdot(lb, lb)
        yield
        n_levels = 5
        for lvl in range(n_levels):
            pwb = pw.astype(BF16)
            if lvl + 1 < n_levels:
                r = _bdot(jnp.concatenate([pwb, x.astype(BF16)], axis=1), pwb)
                pw = r[:, :CHUNK]
                x = x + r[:, CHUNK:]
            else:
                x = x + _bdot(x.astype(BF16), pwb)
            yield
        eg = jnp.exp(gc)
        rhs = jnp.concatenate([v * beta, kb * eg], axis=-1).astype(BF16)
        sol = _bdot(x.astype(BF16), rhs)
        yield
        solb = sol.astype(BF16)
        iuw = _bdot(intra.astype(BF16), solb)
        yield
        kd = k * jnp.exp(g_last - gc)
        kuw = _bdot(jnp.swapaxes(kd, 1, 2).astype(BF16), solb)
        out.update(
            qs=(q * eg - iuw[:, :, hd:]).astype(BF16), o0=iuw[:, :, :hd],
            m=kuw[:, :, hd:].astype(BF16), bb=kuw[:, :, :hd],
            egl=jnp.broadcast_to(jnp.exp(g_last), (nc * DN_HEADS, 1, hd)))
        yield

    def scan_steps(c0, nc, ops, state):
        for n in range(nc):
            r0 = (c0 + n) * CHUNK
            for h in range(DN_HEADS):
                b = h * nc + n
                s = state[h]
                sb = s.astype(BF16)
                ob_ref[r0:r0 + CHUNK, h * hd:(h + 1) * hd] = _dot(ops["qs"][b], sb) + ops["o0"][b]
                state[h] = (s * ops["egl"][b] - _dot(ops["m"][b], sb)) + ops["bb"][b]
            yield

    state = [s_ref[h] for h in range(DN_HEADS)]
    pending = iter(())
    c0 = 0
    for nc in DN_GROUPS:
        ops = {}
        for stage, _ in enumerate(local_stages(c0, nc, ops)):
            if stage % DN_SCAN_EVERY == DN_SCAN_EVERY - 1:
                next(pending, None)
        for _ in pending:
            pass
        pending = scan_steps(c0, nc, ops, state)
        c0 += nc
    for _ in pending:
        pass
    for h in range(DN_HEADS):
        s_ref[h] = state[h]

    for h in range(DN_HEADS):
        hcols = slice(h * hd, (h + 1) * hd)
        o = ob_ref[:, hcols]
        ms = jnp.mean(o * o, axis=-1, keepdims=True)
        o_ref[:, hcols] = ((o * lax.rsqrt(ms + EPS)) * og_ref[:, hcols]) * z_ref[:, hcols]


def _deltanet(dq, dk, dv, zs, gates, og):
    t = dq.shape[0]
    hd = DN_HEAD_DIM
    row = lambda i: (i, 0)
    return pl.pallas_call(
        _deltanet_kernel,
        grid=(t // DN_TB,),
        in_specs=[
            pl.BlockSpec((DN_TB, DN_WIDTH), row), pl.BlockSpec((DN_TB, DN_WIDTH), row),
            pl.BlockSpec((DN_TB, DN_WIDTH), row), pl.BlockSpec((DN_TB, DN_WIDTH), row),
            pl.BlockSpec((DN_TB, 2 * DN_HEADS), row),
            _resident((1, DN_WIDTH)),
        ],
        out_specs=pl.BlockSpec((DN_TB, DN_WIDTH), row),
        out_shape=jax.ShapeDtypeStruct((t, DN_WIDTH), F32),
        scratch_shapes=[
            pltpu.VMEM((DN_HEADS, hd, hd), F32),
            pltpu.VMEM((DN_TB, DN_WIDTH), F32),
        ],
        compiler_params=pltpu.CompilerParams(
            dimension_semantics=("arbitrary",), vmem_limit_bytes=VMEM_LIMIT),
        name="deltanet",
    )(dq, dk, dv, zs, gates, og)


def _out_ffn_kernel(x_ref, a_ref, d_ref, wo_ref, g_ref, wg_ref, wu_ref, wd_ref, o_ref):
    groups = [slice(r, r + FFN_ROWS) for r in range(0, FFN_TM, FFN_ROWS)]

    def out_proj(rows):
        return (x_ref[rows, :] + _dot(a_ref[rows, :].astype(BF16), wo_ref[0:ATTN_WIDTH, :])
                + _dot(d_ref[rows, :].astype(BF16), wo_ref[ATTN_WIDTH:D_MODEL, :]))

    def normed(x1):
        ms = jnp.mean(x1 * x1, axis=-1, keepdims=True)
        return ((x1 * lax.rsqrt(ms + EPS)) * g_ref[...]).astype(BF16)

    x1s = [out_proj(rows) for rows in groups]
    hs = [normed(x1) for x1 in x1s]
    gus = [(_dot(h, wg_ref[...]), _dot(h, wu_ref[...])) for h in hs]
    ys = [(_silu(gate) * up).astype(BF16) for gate, up in gus]
    for rows, x1, y in zip(groups, x1s, ys):
        o_ref[rows, :] = x1 + _dot(y, wd_ref[...])


def _out_ffn(x2, a_out, d_out, wo, g, wg, wu, wd):
    t = x2.shape[0]
    row = lambda i: (i, 0)
    return pl.pallas_call(
        _out_ffn_kernel,
        grid=(t // FFN_TM,),
        in_specs=[
            pl.BlockSpec((FFN_TM, D_MODEL), row),
            pl.BlockSpec((FFN_TM, ATTN_WIDTH), row),
            pl.BlockSpec((FFN_TM, DN_WIDTH), row),
            _resident(wo.shape), _resident((1, D_MODEL)),
            _resident(wg.shape), _resident(wu.shape), _resident(wd.shape),
        ],
        out_specs=pl.BlockSpec((FFN_TM, D_MODEL), row),
        out_shape=jax.ShapeDtypeStruct((t, D_MODEL), F32),
        compiler_params=pltpu.CompilerParams(
            dimension_semantics=("arbitrary",), vmem_limit_bytes=VMEM_LIMIT),
        name="out_ffn",
    )(x2, a_out, d_out, wo, g, wg, wu, wd)


def _layer(x2, norm_mix_g, w_in, attn_q_norm_g, attn_k_norm_g, rel_bias, attn_out_norm_g,
           conv_w, a_log, dt_bias, dn_out_norm_g, w_out, norm_ffn_g, w_gate, w_up, w_down):
    alog_row = jnp.pad(a_log, (DN_HEADS, 0))[None, :]
    dtb_row = jnp.pad(dt_bias, (DN_HEADS, 0))[None, :]
    a_qkv, dq, dk, dv, zs, gates = _in_proj(
        x2, norm_mix_g[None, :], w_in.astype(BF16), conv_w, alog_row, dtb_row)

    qg = jnp.tile(attn_q_norm_g, ATTN_HEADS)[None, :]
    kg = jnp.tile(attn_k_norm_g, ATTN_HEADS)[None, :]
    rb_t = rel_bias.T
    far = jnp.broadcast_to(rb_t[:, 2 * REL_CLIP:], (ATTN_HEADS, WIN + PAIR - 2 * REL_CLIP))
    e_tab = jnp.concatenate([far, rb_t[:, 2 * REL_CLIP:0:-1]], axis=1)
    hid = jnp.arange(ATTN_WIDTH) // ATTN_HEAD_DIM
    bd = (hid[:, None] == hid[None, :]).astype(BF16)
    a_out = _attention(a_qkv, qg, kg, e_tab, bd, attn_out_norm_g[None, :])

    og = jnp.tile(dn_out_norm_g, DN_HEADS)[None, :]
    d_out = _deltanet(dq, dk, dv, zs, gates, og)

    return _out_ffn(x2, a_out, d_out, w_out.astype(BF16), norm_ffn_g[None, :],
                    w_gate.astype(BF16), w_up.astype(BF16), w_down.astype(BF16))


def kernel(x, norm_mix_g, w_in, attn_q_norm_g, attn_k_norm_g, rel_bias, attn_out_norm_g, conv_w, a_log, dt_bias, dn_out_norm_g, w_out, norm_ffn_g, w_gate, w_up, w_down):
    b, t, d = x.shape
    outs = []
    for bi in range(b):
        x2 = x[bi]
        for l in range(w_in.shape[0]):
            x2 = _layer(x2, norm_mix_g[l], w_in[l], attn_q_norm_g[l], attn_k_norm_g[l], rel_bias[l],
                        attn_out_norm_g[l], conv_w[l], a_log[l], dt_bias[l], dn_out_norm_g[l],
                        w_out[l], norm_ffn_g[l], w_gate[l], w_up[l], w_down[l])
        outs.append(x2)
    return jnp.stack(outs, axis=0) if b > 1 else outs[0][None]
```

```python
import functools

import jax
import jax.numpy as jnp
from jax import lax
from jax.experimental import pallas as pl
from jax.experimental.pallas import tpu as pltpu

F32 = jnp.float32
BF16 = jnp.bfloat16

D_MODEL = 1024
CHUNK = 64
ATTN_HEAD_DIM = 64
ATTN_HEADS = 8
ATTN_WIDTH = ATTN_HEADS * ATTN_HEAD_DIM
LEFT_CHUNKS = 8
REL_CLIP = 128
DN_HEAD_DIM = 128
DN_HEADS = 4
DN_WIDTH = DN_HEADS * DN_HEAD_DIM
CONV_K = 4
EPS = 1e-6
NEG_INF = -1e30
LOG2E = 1.4426950408889634

LANES = 128
SUBLANES = 8
MXU_N = 256
VMEM_LIMIT = 56 * 1024 * 1024

PROJ_TM = 512
CONV_ROWS = 256
ATTN_TB = 1024
ATTN_PREV = LEFT_CHUNKS * CHUNK
PAIR = 2 * CHUNK
WIN = PAIR + LEFT_CHUNKS * CHUNK
DN_TB = 1024
DN_GROUPS = (4, 4, 4, 4)
DN_SCAN_EVERY = 1
FFN_TM = 1024
FFN_ROWS = 256


def _resident(shape):
    return pl.BlockSpec(shape, lambda i: (0,) * len(shape), pipeline_mode=pl.Buffered(1))


def _dot(a, b):
    return jnp.dot(a, b, preferred_element_type=F32)


def _dot_nt(a, b):
    return lax.dot_general(a, b, (((1,), (1,)), ((), ())), preferred_element_type=F32)


def _bdot(a, b):
    return lax.dot_general(a, b, (((2,), (1,)), ((0,), (0,))), preferred_element_type=F32)


def _bdot_nt(a, b):
    return lax.dot_general(a, b, (((2,), (2,)), ((0,), (0,))), preferred_element_type=F32)


def _silu(x):
    h = 0.5 * x
    return h + h * jnp.tanh(h)


def _in_proj_kernel(x_ref, xn_ref, g_ref, w_ref, cw_ref, alog_ref, dtb_ref,
                    a_ref, dqkv_ref, z_ref, gates_ref, h0_ref, h1_ref, *buf_refs):
    i = pl.program_id(0)
    hd = DN_HEAD_DIM
    pad = SUBLANES
    col_d = 3 * ATTN_WIDTH
    col_z = col_d + 3 * DN_WIDTH
    col_g = col_z + DN_WIDTH

    def normed(x):
        ms = jnp.mean(x * x, axis=-1, keepdims=True)
        return ((x * lax.rsqrt(ms + EPS)) * g_ref[...]).astype(BF16)

    @pl.when(i == 0)
    def _first():
        for buf_ref in buf_refs:
            buf_ref[0:pad, :] = jnp.zeros((pad, MXU_N), F32)
        h0_ref[...] = normed(x_ref[...])

    @pl.when(i > 0)
    def _carry_tail():
        for buf_ref in buf_refs:
            buf_ref[0:pad, :] = buf_ref[PROJ_TM:PROJ_TM + pad, :]

    def conv_tile(c):
        cols = slice(c * hd, (c + 1) * hd)
        buf_ref = buf_refs[c * hd // MXU_N]
        bcols = slice(c * hd % MXU_N, c * hd % MXU_N + hd)
        part = c // DN_HEADS
        for r0 in range(0, PROJ_TM, CONV_ROWS):
            rows = slice(r0, r0 + CONV_ROWS)
            y = jnp.zeros((CONV_ROWS, hd), F32)
            for tap in range(CONV_K):
                lo = pad - (CONV_K - 1) + tap + r0
                y = y + cw_ref[tap:tap + 1, cols] * buf_ref[lo:lo + CONV_ROWS, bcols]
            y = _silu(y)
            if part < 2:
                y = y * lax.rsqrt(jnp.sum(y * y, axis=-1, keepdims=True) + EPS)
            if part == 0:
                y = y * (hd ** -0.5)
            dqkv_ref[rows, cols] = y

    def project(h_ref, h_next_ref):
        h_next_ref[...] = normed(xn_ref[...])
        h = h_ref[...]

        ba = _dot(h, w_ref[:, col_g:col_g + 2 * DN_HEADS])
        beta = jax.nn.sigmoid(ba)
        y = ba + dtb_ref[...]
        g = -jnp.exp(alog_ref[...]) * (jnp.maximum(y, 0.0) + jnp.log1p(jnp.exp(-jnp.abs(y))))
        pos = lax.broadcasted_iota(jnp.int32, g.shape, 0) % CHUNK
        shift = 1
        while shift < CHUNK:
            g = g + jnp.where(pos >= shift, pltpu.roll(g, shift, 0), 0.0)
            shift *= 2
        lane = lax.broadcasted_iota(jnp.int32, g.shape, 1)
        gates_ref[...] = jnp.where(lane < DN_HEADS, beta, g)

        for j in range(3 * DN_WIDTH // MXU_N):
            cols = slice(j * MXU_N, (j + 1) * MXU_N)
            buf_refs[j][pad:pad + PROJ_TM, :] = _dot(h, w_ref[:, col_d + j * MXU_N:col_d + (j + 1) * MXU_N])
            a_ref[:, cols] = _dot(h, w_ref[:, cols])
            for c in range(j * MXU_N // hd, (j + 1) * MXU_N // hd):
                conv_tile(c)
        z_ref[...] = _silu(_dot(h, w_ref[:, col_z:col_g]))

    @pl.when(i % 2 == 0)
    def _even():
        project(h0_ref, h1_ref)

    @pl.when(i % 2 == 1)
    def _odd():
        project(h1_ref, h0_ref)


def _in_proj(x2, g, wb, conv_w, alog_row, dtb_row):
    t = x2.shape[0]
    n = t // PROJ_TM
    row = lambda i: (i, 0)
    widths = (3 * ATTN_WIDTH, 3 * DN_WIDTH, DN_WIDTH, 2 * DN_HEADS)
    return pl.pallas_call(
        _in_proj_kernel,
        grid=(n,),
        in_specs=[
            pl.BlockSpec((PROJ_TM, D_MODEL), row),
            pl.BlockSpec((PROJ_TM, D_MODEL), lambda i: (jnp.minimum(i + 1, n - 1), 0)),
            _resident((1, D_MODEL)), _resident(wb.shape),
            _resident(conv_w.shape), _resident((1, 2 * DN_HEADS)), _resident((1, 2 * DN_HEADS)),
        ],
        out_specs=[pl.BlockSpec((PROJ_TM, w), row) for w in widths],
        out_shape=[jax.ShapeDtypeStruct((t, w), F32) for w in widths],
        scratch_shapes=[pltpu.VMEM((PROJ_TM, D_MODEL), BF16), pltpu.VMEM((PROJ_TM, D_MODEL), BF16)] + [
            pltpu.VMEM((PROJ_TM + 2 * SUBLANES, MXU_N), F32) for _ in range(3 * DN_WIDTH // MXU_N)],
        compiler_params=pltpu.CompilerParams(
            dimension_semantics=("arbitrary",), vmem_limit_bytes=VMEM_LIMIT),
        name="in_proj",
    )(x2, x2, g, wb, conv_w, alog_row, dtb_row)


def _head_mean_sq(x, bd):
    return _dot((x * x).astype(BF16), bd) * (1.0 / ATTN_HEAD_DIM)


def _attn_kernel(q_ref, k_ref, v_ref, qg_ref, kg_ref, e_ref, bd_ref, og_ref, o_ref,
                 bias_ref, qn_ref, kn_ref, vn_ref, ob_ref):
    i = pl.program_id(0)
    n_hp = ATTN_HEADS // 2

    @pl.when(i == 0)
    def _init():
        r = lax.broadcasted_iota(jnp.int32, (PAIR, WIN), 0)
        j = lax.broadcasted_iota(jnp.int32, (PAIR, WIN), 1)
        first_key = jnp.where(r < CHUNK, 0, CHUNK)
        valid = (j >= first_key) & (j < first_key + WIN - CHUNK)
        for h in range(ATTN_HEADS):
            eb = jnp.broadcast_to(e_ref[h:h + 1, :], (PAIR, WIN + PAIR))
            rolled = pltpu.roll(eb, 0, 1, stride=1, stride_axis=0)
            tab = jnp.where(valid, rolled[:, PAIR:] * LOG2E, NEG_INF)
            bias_ref[h // 2, (h % 2) * PAIR:(h % 2 + 1) * PAIR, :] = tab
        kn_ref[0:ATTN_PREV, :] = jnp.zeros((ATTN_PREV, ATTN_WIDTH), BF16)
        vn_ref[0:ATTN_PREV, :] = jnp.zeros((ATTN_PREV, ATTN_WIDTH), BF16)

    @pl.when(i > 0)
    def _shift():
        kn_ref[0:ATTN_PREV, :] = kn_ref[ATTN_TB:ATTN_TB + ATTN_PREV, :]
        vn_ref[0:ATTN_PREV, :] = vn_ref[ATTN_TB:ATTN_TB + ATTN_PREV, :]

    bd = bd_ref[...]
    q = q_ref[...]
    qn = (q * lax.rsqrt(_head_mean_sq(q, bd) + EPS)) * qg_ref[...]
    qn_ref[...] = (qn * (ATTN_HEAD_DIM ** -0.5 * LOG2E)).astype(BF16)
    k = k_ref[...]
    kn = (k * lax.rsqrt(_head_mean_sq(k, bd) + EPS)) * kg_ref[...]
    kn_ref[ATTN_PREV:ATTN_PREV + ATTN_TB, :] = kn.astype(BF16)
    vn_ref[ATTN_PREV:ATTN_PREV + ATTN_TB, :] = v_ref[...].astype(BF16)

    lane = lax.broadcasted_iota(jnp.int32, (PAIR, LANES), 1)
    first_half = lane < ATTN_HEAD_DIM

    def scores(q0, hp, first_block):
        cols = slice(hp * LANES, (hp + 1) * LANES)
        q2 = qn_ref[pl.ds(q0, PAIR), cols]
        zero = jnp.zeros_like(q2)
        lhs = jnp.concatenate(
            [jnp.where(first_half, q2, zero), jnp.where(first_half, zero, q2)], axis=0)
        s = _dot_nt(lhs, kn_ref[pl.ds(q0, WIN), cols]) + bias_ref[hp]
        if first_block:
            jj = lax.broadcasted_iota(jnp.int32, (2 * PAIR, WIN), 1)
            s = jnp.where(jj + q0 >= ATTN_PREV, s, NEG_INF)
        return s

    def finish(s, q0, hp):
        cols = slice(hp * LANES, (hp + 1) * LANES)
        m = jnp.max(s, axis=-1, keepdims=True)
        e = jnp.exp2(s - m)
        l = jnp.sum(e, axis=-1, keepdims=True)
        pv = _dot(e.astype(BF16), vn_ref[pl.ds(q0, WIN), cols]) / l
        ob_ref[pl.ds(q0, PAIR), cols] = jnp.where(first_half, pv[:PAIR], pv[PAIR:])

    @pl.when(i == 0)
    def _first():
        def pair_body(p, carry):
            q0 = pl.multiple_of(p * PAIR, PAIR)
            for hp in range(n_hp):
                finish(scores(q0, hp, True), q0, hp)
            return carry
        lax.fori_loop(0, ATTN_TB // PAIR, pair_body, 0)

    @pl.when(i > 0)
    def _rest():
        items = [(p * PAIR, hp) for p in range(ATTN_TB // PAIR) for hp in range(n_hp)]
        s_next = scores(*items[0], False)
        for j, item in enumerate(items):
            s_cur = s_next
            if j + 1 < len(items):
                s_next = scores(*items[j + 1], False)
            finish(s_cur, *item)

    a = ob_ref[...]
    ms = jnp.mean(a * a, axis=-1, keepdims=True)
    o_ref[...] = (a * lax.rsqrt(ms + EPS)) * og_ref[...]


def _attention(a_qkv, qg, kg, e_tab, bd, og):
    t = a_qkv.shape[0]
    n_hp = ATTN_HEADS // 2
    return pl.pallas_call(
        _attn_kernel,
        grid=(t // ATTN_TB,),
        in_specs=[
            pl.BlockSpec((ATTN_TB, ATTN_WIDTH), lambda i: (i, 0)),
            pl.BlockSpec((ATTN_TB, ATTN_WIDTH), lambda i: (i, 1)),
            pl.BlockSpec((ATTN_TB, ATTN_WIDTH), lambda i: (i, 2)),
            _resident((1, ATTN_WIDTH)), _resident((1, ATTN_WIDTH)),
            _resident(e_tab.shape), _resident(bd.shape), _resident((1, ATTN_WIDTH)),
        ],
        out_specs=pl.BlockSpec((ATTN_TB, ATTN_WIDTH), lambda i: (i, 0)),
        out_shape=jax.ShapeDtypeStruct((t, ATTN_WIDTH), F32),
        scratch_shapes=[
            pltpu.VMEM((n_hp, 2 * PAIR, WIN), F32),
            pltpu.VMEM((ATTN_TB, ATTN_WIDTH), BF16),
            pltpu.VMEM((ATTN_PREV + ATTN_TB, ATTN_WIDTH), BF16),
            pltpu.VMEM((ATTN_PREV + ATTN_TB, ATTN_WIDTH), BF16),
            pltpu.VMEM((ATTN_TB, ATTN_WIDTH), F32),
        ],
        compiler_params=pltpu.CompilerParams(
            dimension_semantics=("arbitrary",), vmem_limit_bytes=VMEM_LIMIT),
        name="band_attn",
    )(a_qkv, a_qkv, a_qkv, qg, kg, e_tab, bd, og)


def _deltanet_kernel(qn_ref, kn_ref, vn_ref, z_ref, gcol_ref, og_ref, o_ref, s_ref, ob_ref):
    i = pl.program_id(0)
    hd = DN_HEAD_DIM

    @pl.when(i == 0)
    def _init():
        s_ref[...] = jnp.zeros(s_ref.shape, F32)

    ri = lax.broadcasted_iota(jnp.int32, (1, CHUNK, CHUNK), 1)
    ci = lax.broadcasted_iota(jnp.int32, (1, CHUNK, CHUNK), 2)
    tri_incl = ri >= ci

    def local_stages(c0, nc, out):
        rows = slice(c0 * CHUNK, (c0 + nc) * CHUNK)

        def per_head(x, width, off=0):
            x3 = x.reshape(nc, CHUNK, x.shape[-1])
            return jnp.concatenate(
                [x3[:, :, off + h * width:off + (h + 1) * width] for h in range(DN_HEADS)], axis=0)

        k = per_head(kn_ref[rows, :], hd)
        q = per_head(qn_ref[rows, :], hd)
        v = per_head(vn_ref[rows, :], hd)
        gcol = gcol_ref[rows, :]
        beta = per_head(gcol, 1)
        gc = per_head(gcol, 1, DN_HEADS)
        grow = jnp.swapaxes(gcol.reshape(nc, CHUNK, 2 * DN_HEADS), 1, 2)
        gcr = jnp.concatenate(
            [grow[:, DN_HEADS + h:DN_HEADS + h + 1, :] for h in range(DN_HEADS)], axis=0)
        g_last = gcr[:, :, CHUNK - 1:CHUNK]
        decay = jnp.where(tri_incl, jnp.exp(jnp.where(tri_incl, gc - gcr, 0.0)), 0.0)
        kb = k * beta
        kbf = k.astype(BF16)
        kkqk = _bdot_nt(jnp.concatenate([kb, q], axis=1).astype(BF16), kbf)
        yield
        lmat = jnp.where(ri > ci, kkqk[:, :CHUNK] * decay, 0.0)
        intra = kkqk[:, CHUNK:] * decay
        x = jnp.where(ri == ci, 1.0, 0.0) - lmat
        lb = lmat.astype(BF16)
        pw = _bdot(lb, lb)
        yield
        n_levels = 5
        for lvl in range(n_levels):
            pwb = pw.astype(BF16)
            if lvl + 1 < n_levels:
                r = _bdot(jnp.concatenate([pwb, x.astype(BF16)], axis=1), pwb)
                pw = r[:, :CHUNK]
                x = x + r[:, CHUNK:]
            else:
                x = x + _bdot(x.astype(BF16), pwb)
            yield
        eg = jnp.exp(gc)
        rhs = jnp.concatenate([v * beta, kb * eg], axis=-1).astype(BF16)
        sol = _bdot(x.astype(BF16), rhs)
        yield
        solb = sol.astype(BF16)
        iuw = _bdot(intra.astype(BF16), solb)
        yield
        kd = k * jnp.exp(g_last - gc)
        kuw = _bdot(jnp.swapaxes(kd, 1, 2).astype(BF16), solb)
        out.update(
            qs=(q * eg - iuw[:, :, hd:]).astype(BF16), o0=iuw[:, :, :hd],
            m=kuw[:, :, hd:].astype(BF16), bb=kuw[:, :, :hd],
            egl=jnp.broadcast_to(jnp.exp(g_last), (nc * DN_HEADS, 1, hd)))
        yield

    def scan_steps(c0, nc, ops, state):
        for n in range(nc):
            r0 = (c0 + n) * CHUNK
            for h in range(DN_HEADS):
                b = h * nc + n
                s = state[h]
                sb = s.astype(BF16)
                ob_ref[r0:r0 + CHUNK, h * hd:(h + 1) * hd] = _dot(ops["qs"][b], sb) + ops["o0"][b]
                state[h] = (s * ops["egl"][b] - _dot(ops["m"][b], sb)) + ops["bb"][b]
            yield

    state = [s_ref[h] for h in range(DN_HEADS)]
    pending = iter(())
    c0 = 0
    for nc in DN_GROUPS:
        ops = {}
        for stage, _ in enumerate(local_stages(c0, nc, ops)):
            if stage % DN_SCAN_EVERY == DN_SCAN_EVERY - 1:
                next(pending, None)
        for _ in pending:
            pass
        pending = scan_steps(c0, nc, ops, state)
        c0 += nc
    for _ in pending:
        pass
    for h in range(DN_HEADS):
        s_ref[h] = state[h]

    for h in range(DN_HEADS):
        hcols = slice(h * hd, (h + 1) * hd)
        o = ob_ref[:, hcols]
        ms = jnp.mean(o * o, axis=-1, keepdims=True)
        o_ref[:, hcols] = ((o * lax.rsqrt(ms + EPS)) * og_ref[:, hcols]) * z_ref[:, hcols]


def _deltanet(dqkv, zs, gates, og):
    t = dqkv.shape[0]
    hd = DN_HEAD_DIM
    row = lambda i: (i, 0)
    return pl.pallas_call(
        _deltanet_kernel,
        grid=(t // DN_TB,),
        in_specs=[
            pl.BlockSpec((DN_TB, DN_WIDTH), row), pl.BlockSpec((DN_TB, DN_WIDTH), lambda i: (i, 1)),
            pl.BlockSpec((DN_TB, DN_WIDTH), lambda i: (i, 2)), pl.BlockSpec((DN_TB, DN_WIDTH), row),
            pl.BlockSpec((DN_TB, 2 * DN_HEADS), row),
            _resident((1, DN_WIDTH)),
        ],
        out_specs=pl.BlockSpec((DN_TB, DN_WIDTH), row),
        out_shape=jax.ShapeDtypeStruct((t, DN_WIDTH), F32),
        scratch_shapes=[
            pltpu.VMEM((DN_HEADS, hd, hd), F32),
            pltpu.VMEM((DN_TB, DN_WIDTH), F32),
        ],
        compiler_params=pltpu.CompilerParams(
            dimension_semantics=("arbitrary",), vmem_limit_bytes=VMEM_LIMIT),
        name="deltanet",
    )(dqkv, dqkv, dqkv, zs, gates, og)


def _out_ffn_kernel(x_ref, a_ref, d_ref, wo_ref, g_ref, wg_ref, wu_ref, wd_ref, o_ref):
    groups = [slice(r, r + FFN_ROWS) for r in range(0, FFN_TM, FFN_ROWS)]

    def out_proj(rows):
        return (x_ref[rows, :] + _dot(a_ref[rows, :].astype(BF16), wo_ref[0:ATTN_WIDTH, :])
                + _dot(d_ref[rows, :].astype(BF16), wo_ref[ATTN_WIDTH:D_MODEL, :]))

    def normed(x1):
        ms = jnp.mean(x1 * x1, axis=-1, keepdims=True)
        return ((x1 * lax.rsqrt(ms + EPS)) * g_ref[...]).astype(BF16)

    x1s = [out_proj(rows) for rows in groups]
    hs = [normed(x1) for x1 in x1s]
    gus = [(_dot(h, wg_ref[...]), _dot(h, wu_ref[...])) for h in hs]
    ys = [(_silu(gate) * up).astype(BF16) for gate, up in gus]
    for rows, x1, y in zip(groups, x1s, ys):
        o_ref[rows, :] = x1 + _dot(y, wd_ref[...])


def _out_ffn(x2, a_out, d_out, wo, g, wg, wu, wd):
    t = x2.shape[0]
    row = lambda i: (i, 0)
    return pl.pallas_call(
        _out_ffn_kernel,
        grid=(t // FFN_TM,),
        in_specs=[
            pl.BlockSpec((FFN_TM, D_MODEL), row),
            pl.BlockSpec((FFN_TM, ATTN_WIDTH), row),
            pl.BlockSpec((FFN_TM, DN_WIDTH), row),
            _resident(wo.shape), _resident((1, D_MODEL)),
            _resident(wg.shape), _resident(wu.shape), _resident(wd.shape),
        ],
        out_specs=pl.BlockSpec((FFN_TM, D_MODEL), row),
        out_shape=jax.ShapeDtypeStruct((t, D_MODEL), F32),
        compiler_params=pltpu.CompilerParams(
            dimension_semantics=("arbitrary",), vmem_limit_bytes=VMEM_LIMIT),
        name="out_ffn",
    )(x2, a_out, d_out, wo, g, wg, wu, wd)


def _layer(x2, norm_mix_g, w_in, attn_q_norm_g, attn_k_norm_g, rel_bias, attn_out_norm_g,
           conv_w, a_log, dt_bias, dn_out_norm_g, w_out, norm_ffn_g, w_gate, w_up, w_down):
    alog_row = jnp.pad(a_log, (DN_HEADS, 0))[None, :]
    dtb_row = jnp.pad(dt_bias, (DN_HEADS, 0))[None, :]
    a_qkv, dqkv, zs, gates = _in_proj(
        x2, norm_mix_g[None, :], w_in.astype(BF16), conv_w, alog_row, dtb_row)

    qg = jnp.tile(attn_q_norm_g, ATTN_HEADS)[None, :]
    kg = jnp.tile(attn_k_norm_g, ATTN_HEADS)[None, :]
    rb_t = rel_bias.T
    far = jnp.broadcast_to(rb_t[:, 2 * REL_CLIP:], (ATTN_HEADS, WIN + PAIR - 2 * REL_CLIP))
    e_tab = jnp.concatenate([far, rb_t[:, 2 * REL_CLIP:0:-1]], axis=1)
    hid = jnp.arange(ATTN_WIDTH) // ATTN_HEAD_DIM
    bd = (hid[:, None] == hid[None, :]).astype(BF16)
    a_out = _attention(a_qkv, qg, kg, e_tab, bd, attn_out_norm_g[None, :])

    og = jnp.tile(dn_out_norm_g, DN_HEADS)[None, :]
    d_out = _deltanet(dqkv, zs, gates, og)

    return _out_ffn(x2, a_out, d_out, w_out.astype(BF16), norm_ffn_g[None, :],
                    w_gate.astype(BF16), w_up.astype(BF16), w_down.astype(BF16))


def kernel(x, norm_mix_g, w_in, attn_q_norm_g, attn_k_norm_g, rel_bias, attn_out_norm_g, conv_w, a_log, dt_bias, dn_out_norm_g, w_out, norm_ffn_g, w_gate, w_up, w_down):
    b, t, d = x.shape
    outs = []
    for bi in range(b):
        x2 = x[bi]
        for l in range(w_in.shape[0]):
            x2 = _layer(x2, norm_mix_g[l], w_in[l], attn_q_norm_g[l], attn_k_norm_g[l], rel_bias[l],
                        attn_out_norm_g[l], conv_w[l], a_log[l], dt_bias[l], dn_out_norm_g[l],
                        w_out[l], norm_ffn_g[l], w_gate[l], w_up[l], w_down[l])
        outs.append(x2)
    return jnp.stack(outs, axis=0) if b > 1 else outs[0][None]
```

```python
import functools

import jax
import jax.numpy as jnp
from jax import lax
from jax.experimental import pallas as pl
from jax.experimental.pallas import tpu as pltpu

F32 = jnp.float32
BF16 = jnp.bfloat16

D_MODEL = 1024
CHUNK = 64
ATTN_HEAD_DIM = 64
ATTN_HEADS = 8
ATTN_WIDTH = ATTN_HEADS * ATTN_HEAD_DIM
LEFT_CHUNKS = 8
REL_CLIP = 128
DN_HEAD_DIM = 128
DN_HEADS = 4
DN_WIDTH = DN_HEADS * DN_HEAD_DIM
CONV_K = 4
EPS = 1e-6
NEG_INF = -1e30
LOG2E = 1.4426950408889634

LANES = 128
SUBLANES = 8
MXU_N = 256
VMEM_LIMIT = 56 * 1024 * 1024

PROJ_TM = 512
CONV_ROWS = 256
ATTN_TB = 1024
ATTN_PREV = LEFT_CHUNKS * CHUNK
PAIR = 2 * CHUNK
WIN = PAIR + LEFT_CHUNKS * CHUNK
DN_TB = 1024
DN_GROUPS = (4, 4, 4, 4)
DN_SCAN_EVERY = 1
FFN_TM = 1024
FFN_ROWS = 256


def _resident(shape):
    return pl.BlockSpec(shape, lambda i: (0,) * len(shape), pipeline_mode=pl.Buffered(1))


def _dot(a, b):
    return jnp.dot(a, b, preferred_element_type=F32)


def _dot_nt(a, b):
    return lax.dot_general(a, b, (((1,), (1,)), ((), ())), preferred_element_type=F32)


def _bdot(a, b):
    return lax.dot_general(a, b, (((2,), (1,)), ((0,), (0,))), preferred_element_type=F32)


def _bdot_nt(a, b):
    return lax.dot_general(a, b, (((2,), (2,)), ((0,), (0,))), preferred_element_type=F32)


def _silu(x):
    h = 0.5 * x
    return h + h * jnp.tanh(h)


def _in_proj_kernel(x_ref, xn_ref, g_ref, w_ref, cw_ref, alog_ref, dtb_ref,
                    a_ref, dqkv_ref, z_ref, gates_ref, h0_ref, h1_ref, *buf_refs):
    i = pl.program_id(0)
    hd = DN_HEAD_DIM
    pad = SUBLANES
    col_d = 3 * ATTN_WIDTH
    col_z = col_d + 3 * DN_WIDTH
    col_g = col_z + DN_WIDTH

    def normed(x):
        ms = jnp.mean(x * x, axis=-1, keepdims=True)
        return ((x * lax.rsqrt(ms + EPS)) * g_ref[...]).astype(BF16)

    @pl.when(i == 0)
    def _first():
        for buf_ref in buf_refs:
            buf_ref[0:pad, :] = jnp.zeros((pad, MXU_N), F32)
        h0_ref[...] = normed(x_ref[...])

    @pl.when(i > 0)
    def _carry_tail():
        for buf_ref in buf_refs:
            buf_ref[0:pad, :] = buf_ref[PROJ_TM:PROJ_TM + pad, :]

    def conv_tile(c):
        cols = slice(c * hd, (c + 1) * hd)
        buf_ref = buf_refs[c * hd // MXU_N]
        bcols = slice(c * hd % MXU_N, c * hd % MXU_N + hd)
        part = c // DN_HEADS
        for r0 in range(0, PROJ_TM, CONV_ROWS):
            rows = slice(r0, r0 + CONV_ROWS)
            y = jnp.zeros((CONV_ROWS, hd), F32)
            for tap in range(CONV_K):
                lo = pad - (CONV_K - 1) + tap + r0
                y = y + cw_ref[tap:tap + 1, cols] * buf_ref[lo:lo + CONV_ROWS, bcols]
            y = _silu(y)
            if part < 2:
                y = y * lax.rsqrt(jnp.sum(y * y, axis=-1, keepdims=True) + EPS)
            if part == 0:
                y = y * (hd ** -0.5)
            dqkv_ref[rows, cols] = y

    def project(h_ref, h_next_ref):
        h_next_ref[...] = normed(xn_ref[...])
        h = h_ref[...]

        ba = _dot(h, w_ref[:, col_g:col_g + 2 * DN_HEADS])
        beta = jax.nn.sigmoid(ba)
        y = ba + dtb_ref[...]
        g = -jnp.exp(alog_ref[...]) * (jnp.maximum(y, 0.0) + jnp.log1p(jnp.exp(-jnp.abs(y))))
        pos = lax.broadcasted_iota(jnp.int32, g.shape, 0) % CHUNK
        shift = 1
        while shift < CHUNK:
            g = g + jnp.where(pos >= shift, pltpu.roll(g, shift, 0), 0.0)
            shift *= 2
        lane = lax.broadcasted_iota(jnp.int32, g.shape, 1)
        gates_ref[...] = jnp.where(lane < DN_HEADS, beta, g)

        for j in range(3 * DN_WIDTH // MXU_N):
            cols = slice(j * MXU_N, (j + 1) * MXU_N)
            buf_refs[j][pad:pad + PROJ_TM, :] = _dot(h, w_ref[:, col_d + j * MXU_N:col_d + (j + 1) * MXU_N])
            a_ref[:, cols] = _dot(h, w_ref[:, cols])
            for c in range(j * MXU_N // hd, (j + 1) * MXU_N // hd):
                conv_tile(c)
        z_ref[...] = _silu(_dot(h, w_ref[:, col_z:col_g]))

    @pl.when(i % 2 == 0)
    def _even():
        project(h0_ref, h1_ref)

    @pl.when(i % 2 == 1)
    def _odd():
        project(h1_ref, h0_ref)


def _in_proj(x2, g, wb, conv_w, alog_row, dtb_row):
    t = x2.shape[0]
    n = t // PROJ_TM
    row = lambda i: (i, 0)
    widths = (3 * ATTN_WIDTH, 3 * DN_WIDTH, DN_WIDTH, 2 * DN_HEADS)
    return pl.pallas_call(
        _in_proj_kernel,
        grid=(n,),
        in_specs=[
            pl.BlockSpec((PROJ_TM, D_MODEL), row),
            pl.BlockSpec((PROJ_TM, D_MODEL), lambda i: (jnp.minimum(i + 1, n - 1), 0)),
            _resident((1, D_MODEL)), _resident(wb.shape),
            _resident(conv_w.shape), _resident((1, 2 * DN_HEADS)), _resident((1, 2 * DN_HEADS)),
        ],
        out_specs=[pl.BlockSpec((PROJ_TM, w), row) for w in widths],
        out_shape=[jax.ShapeDtypeStruct((t, w), F32) for w in widths],
        scratch_shapes=[pltpu.VMEM((PROJ_TM, D_MODEL), BF16), pltpu.VMEM((PROJ_TM, D_MODEL), BF16)] + [
            pltpu.VMEM((PROJ_TM + 2 * SUBLANES, MXU_N), F32) for _ in range(3 * DN_WIDTH // MXU_N)],
        compiler_params=pltpu.CompilerParams(
            dimension_semantics=("arbitrary",), vmem_limit_bytes=VMEM_LIMIT),
        name="in_proj",
    )(x2, x2, g, wb, conv_w, alog_row, dtb_row)


def _attn_kernel(q_ref, k_ref, v_ref, qg_ref, kg_ref, e_ref, bd_ref, og_ref, o_ref,
                 bias_ref, qn_ref, kt_ref, vn_ref, ob_ref):
    i = pl.program_id(0)
    n_hp = ATTN_HEADS // 2

    @pl.when(i == 0)
    def _init():
        r = lax.broadcasted_iota(jnp.int32, (PAIR, WIN), 0)
        j = lax.broadcasted_iota(jnp.int32, (PAIR, WIN), 1)
        first_key = jnp.where(r < CHUNK, 0, CHUNK)
        valid = (j >= first_key) & (j < first_key + WIN - CHUNK)
        for h in range(ATTN_HEADS):
            eb = jnp.broadcast_to(e_ref[h:h + 1, :], (PAIR, WIN + PAIR))
            rolled = pltpu.roll(eb, 0, 1, stride=1, stride_axis=0)
            tab = jnp.where(valid, rolled[:, PAIR:] * LOG2E, NEG_INF)
            bias_ref[h // 2, (h % 2) * PAIR:(h % 2 + 1) * PAIR, :] = tab
        kt_ref[:, :, 0:ATTN_PREV] = jnp.zeros((n_hp, LANES, ATTN_PREV), BF16)
        vn_ref[0:ATTN_PREV, :] = jnp.zeros((ATTN_PREV, ATTN_WIDTH), BF16)

    @pl.when(i > 0)
    def _shift():
        kt_ref[:, :, 0:ATTN_PREV] = kt_ref[:, :, ATTN_TB:ATTN_TB + ATTN_PREV]
        vn_ref[0:ATTN_PREV, :] = vn_ref[ATTN_TB:ATTN_TB + ATTN_PREV, :]

    q = q_ref[...]
    q_ms = _dot((q * q).astype(BF16), bd_ref[...])
    qn_ref[...] = ((q * lax.rsqrt(q_ms + EPS)) * qg_ref[...]).astype(BF16)
    for hp in range(n_hp):
        kt = k_ref[:, hp * LANES:(hp + 1) * LANES].T
        halves = []
        for hh in range(2):
            kh = kt[hh * ATTN_HEAD_DIM:(hh + 1) * ATTN_HEAD_DIM, :]
            ms = jnp.sum(kh * kh, axis=0, keepdims=True) * (1.0 / ATTN_HEAD_DIM)
            halves.append(kh * lax.rsqrt(ms + EPS))
        ktn = jnp.concatenate(halves, axis=0) * kg_ref[hp * LANES:(hp + 1) * LANES, :]
        kt_ref[hp, :, ATTN_PREV:ATTN_PREV + ATTN_TB] = ktn.astype(BF16)
    vn_ref[ATTN_PREV:ATTN_PREV + ATTN_TB, :] = v_ref[...].astype(BF16)

    lane = lax.broadcasted_iota(jnp.int32, (PAIR, LANES), 1)
    first_half = lane < ATTN_HEAD_DIM

    def scores(q0, hp, first_block):
        cols = slice(hp * LANES, (hp + 1) * LANES)
        q2 = qn_ref[pl.ds(q0, PAIR), cols]
        zero = jnp.zeros_like(q2)
        lhs = jnp.concatenate(
            [jnp.where(first_half, q2, zero), jnp.where(first_half, zero, q2)], axis=0)
        s = _dot(lhs, kt_ref[hp, :, q0:q0 + WIN]) + bias_ref[hp]
        if first_block and q0 < ATTN_PREV:
            jj = lax.broadcasted_iota(jnp.int32, (2 * PAIR, WIN), 1)
            s = jnp.where(jj + q0 >= ATTN_PREV, s, NEG_INF)
        return s

    def finish(s, q0, hp):
        cols = slice(hp * LANES, (hp + 1) * LANES)
        m = jnp.max(s, axis=-1, keepdims=True)
        e = jnp.exp2(s - m)
        l = jnp.sum(e, axis=-1, keepdims=True)
        pv = _dot(e.astype(BF16), vn_ref[pl.ds(q0, WIN), cols]) / l
        ob_ref[pl.ds(q0, PAIR), cols] = jnp.where(first_half, pv[:PAIR], pv[PAIR:])

    def all_items(first_block):
        items = [(p * PAIR, hp) for p in range(ATTN_TB // PAIR) for hp in range(n_hp)]
        s_next = scores(*items[0], first_block)
        for j, item in enumerate(items):
            s_cur = s_next
            if j + 1 < len(items):
                s_next = scores(*items[j + 1], first_block)
            finish(s_cur, *item)

    @pl.when(i == 0)
    def _first():
        all_items(True)

    @pl.when(i > 0)
    def _rest():
        all_items(False)

    a = ob_ref[...]
    ms = jnp.mean(a * a, axis=-1, keepdims=True)
    o_ref[...] = (a * lax.rsqrt(ms + EPS)) * og_ref[...]


def _attention(a_qkv, qg, kg, e_tab, bd, og):
    t = a_qkv.shape[0]
    n_hp = ATTN_HEADS // 2
    return pl.pallas_call(
        _attn_kernel,
        grid=(t // ATTN_TB,),
        in_specs=[
            pl.BlockSpec((ATTN_TB, ATTN_WIDTH), lambda i: (i, 0)),
            pl.BlockSpec((ATTN_TB, ATTN_WIDTH), lambda i: (i, 1)),
            pl.BlockSpec((ATTN_TB, ATTN_WIDTH), lambda i: (i, 2)),
            _resident((1, ATTN_WIDTH)), _resident((ATTN_WIDTH, 1)),
            _resident(e_tab.shape), _resident(bd.shape), _resident((1, ATTN_WIDTH)),
        ],
        out_specs=pl.BlockSpec((ATTN_TB, ATTN_WIDTH), lambda i: (i, 0)),
        out_shape=jax.ShapeDtypeStruct((t, ATTN_WIDTH), F32),
        scratch_shapes=[
            pltpu.VMEM((n_hp, 2 * PAIR, WIN), F32),
            pltpu.VMEM((ATTN_TB, ATTN_WIDTH), BF16),
            pltpu.VMEM((n_hp, LANES, ATTN_PREV + ATTN_TB), BF16),
            pltpu.VMEM((ATTN_PREV + ATTN_TB, ATTN_WIDTH), BF16),
            pltpu.VMEM((ATTN_TB, ATTN_WIDTH), F32),
        ],
        compiler_params=pltpu.CompilerParams(
            dimension_semantics=("arbitrary",), vmem_limit_bytes=VMEM_LIMIT),
        name="band_attn",
    )(a_qkv, a_qkv, a_qkv, qg, kg, e_tab, bd, og)


def _deltanet_kernel(qn_ref, kn_ref, vn_ref, z_ref, gcol_ref, og_ref, o_ref, s_ref, ob_ref):
    i = pl.program_id(0)
    hd = DN_HEAD_DIM

    @pl.when(i == 0)
    def _init():
        s_ref[...] = jnp.zeros(s_ref.shape, F32)

    ri = lax.broadcasted_iota(jnp.int32, (1, CHUNK, CHUNK), 1)
    ci = lax.broadcasted_iota(jnp.int32, (1, CHUNK, CHUNK), 2)
    tri_incl = ri >= ci

    def local_stages(c0, nc, out):
        rows = slice(c0 * CHUNK, (c0 + nc) * CHUNK)

        def per_head(x, width, off=0):
            x3 = x.reshape(nc, CHUNK, x.shape[-1])
            return jnp.concatenate(
                [x3[:, :, off + h * width:off + (h + 1) * width] for h in range(DN_HEADS)], axis=0)

        k = per_head(kn_ref[rows, :], hd)
        q = per_head(qn_ref[rows, :], hd)
        v = per_head(vn_ref[rows, :], hd)
        gcol = gcol_ref[rows, :]
        beta = per_head(gcol, 1)
        gc = per_head(gcol, 1, DN_HEADS)
        grow = jnp.swapaxes(gcol.reshape(nc, CHUNK, 2 * DN_HEADS), 1, 2)
        gcr = jnp.concatenate(
            [grow[:, DN_HEADS + h:DN_HEADS + h + 1, :] for h in range(DN_HEADS)], axis=0)
        g_last = gcr[:, :, CHUNK - 1:CHUNK]
        decay = jnp.where(tri_incl, jnp.exp(jnp.where(tri_incl, gc - gcr, 0.0)), 0.0)
        kb = k * beta
        kbf = k.astype(BF16)
        kkqk = _bdot_nt(jnp.concatenate([kb, q], axis=1).astype(BF16), kbf)
        yield
        lmat = jnp.where(ri > ci, kkqk[:, :CHUNK] * decay, 0.0)
        intra = kkqk[:, CHUNK:] * decay
        x = jnp.where(ri == ci, 1.0, 0.0) - lmat
        lb = lmat.astype(BF16)
        pw = _bdot(lb, lb)
        yield
        n_levels = 5
        for lvl in range(n_levels):
            pwb = pw.astype(BF16)
            if lvl + 1 < n_levels:
                r = _bdot(jnp.concatenate([pwb, x.astype(BF16)], axis=1), pwb)
                pw = r[:, :CHUNK]
                x = x + r[:, CHUNK:]
            else:
                x = x + _bdot(x.astype(BF16), pwb)
            yield
        eg = jnp.exp(gc)
        rhs = jnp.concatenate([v * beta, kb * eg], axis=-1).astype(BF16)
        sol = _bdot(x.astype(BF16), rhs)
        yield
        solb = sol.astype(BF16)
        iuw = _bdot(intra.astype(BF16), solb)
        yield
        kd = k * jnp.exp(g_last - gc)
        kuw = _bdot(jnp.swapaxes(kd, 1, 2).astype(BF16), solb)
        out.update(
            qs=(q * eg - iuw[:, :, hd:]).astype(BF16), o0=iuw[:, :, :hd],
            m=kuw[:, :, hd:].astype(BF16), bb=kuw[:, :, :hd],
            egl=jnp.broadcast_to(jnp.exp(g_last), (nc * DN_HEADS, 1, hd)))
        yield

    def scan_steps(c0, nc, ops, state):
        for n in range(nc):
            r0 = (c0 + n) * CHUNK
            for h in range(DN_HEADS):
                b = h * nc + n
                s = state[h]
                sb = s.astype(BF16)
                ob_ref[r0:r0 + CHUNK, h * hd:(h + 1) * hd] = _dot(ops["qs"][b], sb) + ops["o0"][b]
                state[h] = (s * ops["egl"][b] - _dot(ops["m"][b], sb)) + ops["bb"][b]
            yield

    state = [s_ref[h] for h in range(DN_HEADS)]
    pending = iter(())
    c0 = 0
    for nc in DN_GROUPS:
        ops = {}
        for stage, _ in enumerate(local_stages(c0, nc, ops)):
            if stage % DN_SCAN_EVERY == DN_SCAN_EVERY - 1:
                next(pending, None)
        for _ in pending:
            pass
        pending = scan_steps(c0, nc, ops, state)
        c0 += nc
    for _ in pending:
        pass
    for h in range(DN_HEADS):
        s_ref[h] = state[h]

    for h in range(DN_HEADS):
        hcols = slice(h * hd, (h + 1) * hd)
        o = ob_ref[:, hcols]
        ms = jnp.mean(o * o, axis=-1, keepdims=True)
        o_ref[:, hcols] = ((o * lax.rsqrt(ms + EPS)) * og_ref[:, hcols]) * z_ref[:, hcols]


def _deltanet(dqkv, zs, gates, og):
    t = dqkv.shape[0]
    hd = DN_HEAD_DIM
    row = lambda i: (i, 0)
    return pl.pallas_call(
        _deltanet_kernel,
        grid=(t // DN_TB,),
        in_specs=[
            pl.BlockSpec((DN_TB, DN_WIDTH), row), pl.BlockSpec((DN_TB, DN_WIDTH), lambda i: (i, 1)),
            pl.BlockSpec((DN_TB, DN_WIDTH), lambda i: (i, 2)), pl.BlockSpec((DN_TB, DN_WIDTH), row),
            pl.BlockSpec((DN_TB, 2 * DN_HEADS), row),
            _resident((1, DN_WIDTH)),
        ],
        out_specs=pl.BlockSpec((DN_TB, DN_WIDTH), row),
        out_shape=jax.ShapeDtypeStruct((t, DN_WIDTH), F32),
        scratch_shapes=[
            pltpu.VMEM((DN_HEADS, hd, hd), F32),
            pltpu.VMEM((DN_TB, DN_WIDTH), F32),
        ],
        compiler_params=pltpu.CompilerParams(
            dimension_semantics=("arbitrary",), vmem_limit_bytes=VMEM_LIMIT),
        name="deltanet",
    )(dqkv, dqkv, dqkv, zs, gates, og)


def _out_ffn_kernel(x_ref, a_ref, d_ref, wo_ref, g_ref, wg_ref, wu_ref, wd_ref, o_ref):
    groups = [slice(r, r + FFN_ROWS) for r in range(0, FFN_TM, FFN_ROWS)]

    def out_proj(rows):
        return (x_ref[rows, :] + _dot(a_ref[rows, :].astype(BF16), wo_ref[0:ATTN_WIDTH, :])
                + _dot(d_ref[rows, :].astype(BF16), wo_ref[ATTN_WIDTH:D_MODEL, :]))

    def normed(x1):
        ms = jnp.mean(x1 * x1, axis=-1, keepdims=True)
        return ((x1 * lax.rsqrt(ms + EPS)) * g_ref[...]).astype(BF16)

    x1s = [out_proj(rows) for rows in groups]
    hs = [normed(x1) for x1 in x1s]
    gus = [(_dot(h, wg_ref[...]), _dot(h, wu_ref[...])) for h in hs]
    ys = [(_silu(gate) * up).astype(BF16) for gate, up in gus]
    for rows, x1, y in zip(groups, x1s, ys):
        o_ref[rows, :] = x1 + _dot(y, wd_ref[...])


def _out_ffn(x2, a_out, d_out, wo, g, wg, wu, wd):
    t = x2.shape[0]
    row = lambda i: (i, 0)
    return pl.pallas_call(
        _out_ffn_kernel,
        grid=(t // FFN_TM,),
        in_specs=[
            pl.BlockSpec((FFN_TM, D_MODEL), row),
            pl.BlockSpec((FFN_TM, ATTN_WIDTH), row),
            pl.BlockSpec((FFN_TM, DN_WIDTH), row),
            _resident(wo.shape), _resident((1, D_MODEL)),
            _resident(wg.shape), _resident(wu.shape), _resident(wd.shape),
        ],
        out_specs=pl.BlockSpec((FFN_TM, D_MODEL), row),
        out_shape=jax.ShapeDtypeStruct((t, D_MODEL), F32),
        compiler_params=pltpu.CompilerParams(
            dimension_semantics=("arbitrary",), vmem_limit_bytes=VMEM_LIMIT),
        name="out_ffn",
    )(x2, a_out, d_out, wo, g, wg, wu, wd)


def _layer(x2, norm_mix_g, w_in, attn_q_norm_g, attn_k_norm_g, rel_bias, attn_out_norm_g,
           conv_w, a_log, dt_bias, dn_out_norm_g, w_out, norm_ffn_g, w_gate, w_up, w_down):
    alog_row = jnp.pad(a_log, (DN_HEADS, 0))[None, :]
    dtb_row = jnp.pad(dt_bias, (DN_HEADS, 0))[None, :]
    a_qkv, dqkv, zs, gates = _in_proj(
        x2, norm_mix_g[None, :], w_in.astype(BF16), conv_w, alog_row, dtb_row)

    qg = (jnp.tile(attn_q_norm_g, ATTN_HEADS) * (ATTN_HEAD_DIM ** -0.5 * LOG2E))[None, :]
    kg = jnp.tile(attn_k_norm_g, ATTN_HEADS)[:, None]
    rb_t = rel_bias.T
    far = jnp.broadcast_to(rb_t[:, 2 * REL_CLIP:], (ATTN_HEADS, WIN + PAIR - 2 * REL_CLIP))
    e_tab = jnp.concatenate([far, rb_t[:, 2 * REL_CLIP:0:-1]], axis=1)
    hid = jnp.arange(ATTN_WIDTH) // ATTN_HEAD_DIM
    bd = jnp.where(hid[:, None] == hid[None, :], 1.0 / ATTN_HEAD_DIM, 0.0).astype(BF16)
    a_out = _attention(a_qkv, qg, kg, e_tab, bd, attn_out_norm_g[None, :])

    og = jnp.tile(dn_out_norm_g, DN_HEADS)[None, :]
    d_out = _deltanet(dqkv, zs, gates, og)

    return _out_ffn(x2, a_out, d_out, w_out.astype(BF16), norm_ffn_g[None, :],
                    w_gate.astype(BF16), w_up.astype(BF16), w_down.astype(BF16))


def kernel(x, norm_mix_g, w_in, attn_q_norm_g, attn_k_norm_g, rel_bias, attn_out_norm_g, conv_w, a_log, dt_bias, dn_out_norm_g, w_out, norm_ffn_g, w_gate, w_up, w_down):
    b, t, d = x.shape
    outs = []
    for bi in range(b):
        x2 = x[bi]
        for l in range(w_in.shape[0]):
            x2 = _layer(x2, norm_mix_g[l], w_in[l], attn_q_norm_g[l], attn_k_norm_g[l], rel_bias[l],
                        attn_out_norm_g[l], conv_w[l], a_log[l], dt_bias[l], dn_out_norm_g[l],
                        w_out[l], norm_ffn_g[l], w_gate[l], w_up[l], w_down[l])
        outs.append(x2)
    return jnp.stack(outs, axis=0) if b > 1 else outs[0][None]
```
